```python
import jax, jax.numpy as jnp
from jax import lax
import numpy as np

D_MODEL = 1024
BATCH = 8
SEQ = 2048
DEPTH = 4
DEC_BATCH = 128
DEC_SEQ = 8
PAST_LEN = 16384
PAGE_SIZE = 128

D_MIX = D_MODEL
W_A = D_MIX // 2
W_B = D_MIX - W_A
H_A = 8
H_B = 8
HD_B = W_B // H_B
D_IN = 2 * W_A + 2 * W_B
D_FF = 4 * D_MODEL
CONV_A_WIDTH = 31
CONV_B_WIDTH = 4
RG_C = 8.0
EPS = 1e-6

kernel_name = "hymba_conformer_rglru_adaln_decoder_step"


def rms_norm(x, g):
    xf = x.astype(jnp.float32)
    y = xf * lax.rsqrt(jnp.mean(xf * xf, axis=-1, keepdims=True) + EPS)
    return (y * g.astype(jnp.float32)).astype(x.dtype)


def layer_norm(x, g, b):
    xf = x.astype(jnp.float32)
    mu = jnp.mean(xf, axis=-1, keepdims=True)
    xc = xf - mu
    var = jnp.mean(xc * xc, axis=-1, keepdims=True)
    y = xc * lax.rsqrt(var + EPS) * g.astype(jnp.float32) + b.astype(jnp.float32)
    return y.astype(x.dtype)


def causal_dwconv(x, buf, w, b):
    k, c = w.shape
    xp = jnp.concatenate([buf.astype(x.dtype), x], axis=1)
    y = lax.conv_general_dilated(xp, w[:, None, :].astype(x.dtype), window_strides=(1,),
                                 padding='VALID', dimension_numbers=('NWC', 'WIO', 'NWC'),
                                 feature_group_count=c)
    return y + b, xp[:, xp.shape[1] - (k - 1):]


def rg_lru(x, h0, w_ga, b_ga, w_gx, b_gx, lam, reset_first):
    bsz, t, w = x.shape
    xh = x.reshape(bsz, t, H_B, HD_B)
    gate_a = jnp.einsum('bthi,hij->bthj', xh, w_ga).reshape(bsz, t, w) + b_ga
    gate_x = jnp.einsum('bthi,hij->bthj', xh, w_gx).reshape(bsz, t, w) + b_gx
    log_a = -RG_C * jax.nn.sigmoid(gate_a.astype(jnp.float32)) * jax.nn.softplus(-lam.astype(jnp.float32))
    a = jnp.exp(log_a)
    mult = jnp.sqrt(-jnp.expm1(2.0 * log_a))
    if reset_first:
        mult = mult.at[:, 0].set(1.0)
    bterm = mult * jax.nn.sigmoid(gate_x.astype(jnp.float32)) * x.astype(jnp.float32)
    if h0 is not None:
        bterm = bterm.at[:, 0].add(a[:, 0] * h0.astype(jnp.float32))

    def combine(l, r):
        a1, b1 = l
        a2, b2 = r
        return a1 * a2, a2 * b1 + b2

    _, h = lax.associative_scan(combine, (a, bterm), axis=1)
    return h.astype(x.dtype), h[:, -1].astype(x.dtype)


def trunk(x, c, bufs_a, bufs_b, hs, reset_first, w_ada, b_ada, g_mix, w_in, b_in,
          conv_a_w, conv_a_b, ln_a_g, ln_a_b, conv_b_w, conv_b_b,
          w_gate_a, b_gate_a, w_gate_x, b_gate_x, lru_lambda, w_out,
          g_ffn, w_up, w_down, g_final):
    new_a, new_b, new_h = [], [], []
    sc = jax.nn.silu(c)
    for l in range(DEPTH):
        mod = (jnp.einsum('bd,de->be', sc, w_ada[l]) + b_ada[l])[:, None, :]
        sh_m, sc_m, gt_m, sh_f, sc_f, gt_f = jnp.split(mod, 6, axis=-1)
        h = rms_norm(x, g_mix[l]) * (1 + sc_m) + sh_m
        u = jnp.einsum('btd,de->bte', h, w_in[l]) + b_in[l]
        u_val, u_glu, u_rec, u_y = jnp.split(u, [W_A, 2 * W_A, 2 * W_A + W_B], axis=-1)
        ga = u_val * jax.nn.sigmoid(u_glu)
        ca, nba = causal_dwconv(ga, bufs_a[l], conv_a_w[l], conv_a_b[l])
        ya = jax.nn.silu(layer_norm(ca, ln_a_g[l], ln_a_b[l]))
        cb, nbb = causal_dwconv(u_rec, bufs_b[l], conv_b_w[l], conv_b_b[l])
        hb, hlast = rg_lru(cb, None if hs is None else hs[l], w_gate_a[l], b_gate_a[l],
                           w_gate_x[l], b_gate_x[l], lru_lambda[l], reset_first)
        yb = hb * jax.nn.gelu(u_y)
        mix = jnp.einsum('bte,ed->btd', jnp.concatenate([ya, yb], axis=-1), w_out[l])
        x = x + gt_m * mix
        h2 = rms_norm(x, g_ffn[l]) * (1 + sc_f) + sh_f
        ff = jnp.square(jax.nn.relu(jnp.einsum('btd,df->btf', h2, w_up[l])))
        x = x + gt_f * jnp.einsum('btf,fd->btd', ff, w_down[l])
        new_a.append(nba)
        new_b.append(nbb)
        new_h.append(hlast)
    y = rms_norm(x, g_final)
    return y, jnp.stack(new_a), jnp.stack(new_b), jnp.stack(new_h)


def setup_inputs(seed: int = 0) -> dict:
    key = jax.random.key(seed)
    ks = jax.random.split(key, 40)
    f32 = jnp.float32

    def nrm(k, shape, scale):
        return jax.random.normal(k, shape, f32) * scale

    u = jax.random.uniform(ks[0], (DEPTH, W_B), f32, 0.9, 0.999)
    a_base = u ** (1.0 / RG_C)
    lru_lambda = jnp.log(a_base) - jnp.log1p(-a_base)
    return {
        "x_prompt": nrm(ks[1], (BATCH, SEQ, D_MODEL), 1.0),
        "x_sample": nrm(ks[2], (DEC_BATCH, DEC_SEQ, D_MODEL), 1.0),
        "c_prompt": nrm(ks[3], (BATCH, D_MODEL), 1.0),
        "c_sample": nrm(ks[4], (DEC_BATCH, D_MODEL), 1.0),
        "state_conv_a": nrm(ks[5], (DEPTH, DEC_BATCH, CONV_A_WIDTH - 1, W_A), 1.0),
        "state_conv_b": nrm(ks[6], (DEPTH, DEC_BATCH, CONV_B_WIDTH - 1, W_B), 1.0),
        "state_lru_h": nrm(ks[7], (DEPTH, DEC_BATCH, W_B), 0.5),
        "w_ada": nrm(ks[8], (DEPTH, D_MODEL, 6 * D_MODEL), D_MODEL ** -0.5),
        "b_ada": nrm(ks[9], (DEPTH, 6 * D_MODEL), 0.02),
        "g_mix": 1.0 + nrm(ks[10], (DEPTH, D_MODEL), 0.02),
        "w_in": nrm(ks[11], (DEPTH, D_MODEL, D_IN), D_MODEL ** -0.5),
        "b_in": nrm(ks[12], (DEPTH, D_IN), 0.02),
        "conv_a_w": nrm(ks[13], (DEPTH, CONV_A_WIDTH, W_A), CONV_A_WIDTH ** -0.5),
        "conv_a_b": nrm(ks[14], (DEPTH, W_A), 0.02),
        "ln_a_g": 1.0 + nrm(ks[15], (DEPTH, W_A), 0.02),
        "ln_a_b": nrm(ks[16], (DEPTH, W_A), 0.02),
        "conv_b_w": nrm(ks[17], (DEPTH, CONV_B_WIDTH, W_B), CONV_B_WIDTH ** -0.5),
        "conv_b_b": nrm(ks[18], (DEPTH, W_B), 0.02),
        "w_gate_a": nrm(ks[19], (DEPTH, H_B, HD_B, HD_B), HD_B ** -0.5),
        "b_gate_a": nrm(ks[20], (DEPTH, W_B), 0.02),
        "w_gate_x": nrm(ks[21], (DEPTH, H_B, HD_B, HD_B), HD_B ** -0.5),
        "b_gate_x": nrm(ks[22], (DEPTH, W_B), 0.02),
        "lru_lambda": lru_lambda,
        "w_out": nrm(ks[23], (DEPTH, D_MIX, D_MODEL), D_MIX ** -0.5),
        "g_ffn": 1.0 + nrm(ks[24], (DEPTH, D_MODEL), 0.02),
        "w_up": nrm(ks[25], (DEPTH, D_MODEL, D_FF), D_MODEL ** -0.5),
        "w_down": nrm(ks[26], (DEPTH, D_FF, D_MODEL), D_FF ** -0.5),
        "g_final": 1.0 + nrm(ks[27], (D_MODEL,), 0.02),
    }


def reference(x_prompt, x_sample, c_prompt, c_sample, state_conv_a, state_conv_b, state_lru_h,
              w_ada, b_ada, g_mix, w_in, b_in, conv_a_w, conv_a_b, ln_a_g, ln_a_b,
              conv_b_w, conv_b_b, w_gate_a, b_gate_a, w_gate_x, b_gate_x, lru_lambda,
              w_out, g_ffn, w_up, w_down, g_final):
    bp = x_prompt.shape[0]
    zeros_a = jnp.zeros((DEPTH, bp, CONV_A_WIDTH - 1, W_A), x_prompt.dtype)
    zeros_b = jnp.zeros((DEPTH, bp, CONV_B_WIDTH - 1, W_B), x_prompt.dtype)
    y_prompt, pa, pb, ph = trunk(
        x_prompt, c_prompt, zeros_a, zeros_b, None, True,
        w_ada, b_ada, g_mix, w_in, b_in, conv_a_w, conv_a_b, ln_a_g, ln_a_b,
        conv_b_w, conv_b_b, w_gate_a, b_gate_a, w_gate_x, b_gate_x, lru_lambda,
        w_out, g_ffn, w_up, w_down, g_final)
    y_sample, sa, sb, sh = trunk(
        x_sample, c_sample, state_conv_a, state_conv_b, state_lru_h, False,
        w_ada, b_ada, g_mix, w_in, b_in, conv_a_w, conv_a_b, ln_a_g, ln_a_b,
        conv_b_w, conv_b_b, w_gate_a, b_gate_a, w_gate_x, b_gate_x, lru_lambda,
        w_out, g_ffn, w_up, w_down, g_final)
    return (y_prompt, y_sample, pa, pb, ph, sa, sb, sh)
```

```python
import functools

import jax
import jax.numpy as jnp
from jax import lax
from jax.experimental import pallas as pl
from jax.experimental.pallas import tpu as pltpu

F32 = jnp.float32
BF16 = jnp.bfloat16

SUBLANES = 8
LANES = 128
MXU_DIM = 256

RG_C = 8.0
EPS = 1e-6
GELU_C = 0.7978845608028654

PROMPT_ROWS = 256
SAMPLE_SEQS = 32
FF_CHUNK = 1024
ADA_COLS = 1024
VMEM_LIMIT = 52 * 1024 * 1024


def _sigmoid(x):
    return 1.0 / (1.0 + jnp.exp(-x))


def _silu(x):
    return x * _sigmoid(x)


def _gelu_tanh(x):
    return x * (0.5 * (1.0 + jnp.tanh(GELU_C * (x + 0.044715 * (x * x * x)))))


def _softplus(x):
    return jnp.maximum(x, 0.0) + jnp.log1p(jnp.exp(-jnp.abs(x)))


def _rms(x, g):
    return x * lax.rsqrt(jnp.mean(x * x, axis=-1, keepdims=True) + EPS) * g


def _layer_norm(x, g, b):
    mu = jnp.mean(x, axis=-1, keepdims=True)
    xc = x - mu
    var = jnp.mean(xc * xc, axis=-1, keepdims=True)
    return xc * lax.rsqrt(var + EPS) * g + b


def _dot(a, b):
    return jnp.dot(a, b, preferred_element_type=F32)


def _conv_taps(width, hist_rows):
    lead = hist_rows - (width - 1)
    taps = {}
    for k in range(width):
        taps.setdefault((k + lead) % SUBLANES, []).append((k + lead) // SUBLANES)
    return lead, taps


def _conv_tile(tile_fn, w_ref, lanes, width, hist_rows, prev, row):
    lead, taps = _conv_taps(width, hist_rows)

    def z(r, off):
        acc = None
        for q in taps[r]:
            term = w_ref[SUBLANES * q + r - lead, :, lanes] * tile_fn(q + off)
            acc = term if acc is None else acc + term
        return acc

    shifted = [r for r in sorted(taps) if r != 0]
    new = [z(r, 1) for r in shifted]
    if prev is None:
        prev = [z(r, 0) for r in shifted]
    out = z(0, 0)
    for r, p, n in zip(shifted, prev, new):
        out = out + pltpu.roll(jnp.where(row >= r, p, n), SUBLANES - r, axis=0)
    return out, new


def _conv_rows(buf, w_ref, out_ref, n_rows, width, hist_rows):
    channels = out_ref.shape[-1]
    row = lax.broadcasted_iota(jnp.int32, (SUBLANES, LANES), 0)
    for g in range(channels // LANES):
        lanes = slice(g * LANES, (g + 1) * LANES)

        def tile_at(j):
            return lambda i: buf[pl.ds(pl.multiple_of((j + i) * SUBLANES, SUBLANES), SUBLANES), lanes]

        def step(j, prev):
            out, new = _conv_tile(tile_at(j), w_ref, lanes, width, hist_rows, list(prev), row)
            out_ref[pl.ds(pl.multiple_of(j * SUBLANES, SUBLANES), SUBLANES), lanes] = out
            return tuple(new)

        lead, taps = _conv_taps(width, hist_rows)
        first = []
        for r in sorted(taps):
            if r == 0:
                continue
            acc = None
            for q in taps[r]:
                term = w_ref[SUBLANES * q + r - lead, :, lanes] * tile_at(0)(q)
                acc = term if acc is None else acc + term
            first.append(acc)
        lax.fori_loop(0, n_rows // SUBLANES, step, tuple(first), unroll=4)


def _scan_tile(a, b, row):
    for s in (1, 2, 4):
        keep = row >= s
        a_prev = jnp.where(keep, pltpu.roll(a, s, axis=0), 1.0)
        b_prev = jnp.where(keep, pltpu.roll(b, s, axis=0), 0.0)
        b = a * b_prev + b
        a = a * a_prev
    return a, b


def _scan_rows(a_ref, b_ref, carry_ref, n_rows):
    channels = a_ref.shape[-1]
    row = lax.broadcasted_iota(jnp.int32, (SUBLANES, channels), 0)

    def step(j, h_prev):
        rows = pl.ds(pl.multiple_of(j * SUBLANES, SUBLANES), SUBLANES)
        a_cum, b_cum = _scan_tile(a_ref[rows, :], b_ref[rows, :], row)
        h = a_cum * h_prev + b_cum
        b_ref[rows, :] = h
        return jnp.broadcast_to(h[SUBLANES - 1:SUBLANES, :], (SUBLANES, channels))

    carry_ref[...] = lax.fori_loop(0, n_rows // SUBLANES, step, carry_ref[...], unroll=2)


def _lru_inputs(cb, g, b_ga, b_gx, softplus_neg_lam, first_row):
    n = cb.shape[-1]
    gate_a = g[:, 0:n] + b_ga
    gate_x = g[:, n:2 * n] + b_gx
    log_a = -RG_C * _sigmoid(gate_a) * softplus_neg_lam
    a = jnp.exp(log_a)
    mult = jnp.sqrt(-jnp.tanh(log_a) * (a * a + 1.0))
    if first_row is not None:
        mult = jnp.where(first_row, 1.0, mult)
    return a, mult * _sigmoid(gate_x) * cb


def _mixer_head(x, shift, scale, w):
    h = (_rms(x, w["g_mix"][...]) * (1.0 + scale) + shift).astype(BF16)

    def proj(c0, c1):
        return _dot(h, w["w_in"][:, c0:c1]) + w["b_in"][:, c0:c1]

    return proj


def _mixer_gates(w, cb_ref, a_ref, b_ref, first_row):
    cb = cb_ref[...] + w["conv_b_b"][...]
    cb_lo = cb.astype(BF16)
    sp = _softplus(-w["lam"][...])
    for kb in range(cb.shape[-1] // MXU_DIM):
        cs = slice(kb * MXU_DIM, (kb + 1) * MXU_DIM)
        g = _dot(cb_lo[:, cs], w["w_gate"][kb])
        a, b = _lru_inputs(cb[:, cs], g, w["b_ga"][:, cs], w["b_gx"][:, cs], sp[:, cs], first_row)
        a_ref[:, cs] = a
        b_ref[:, cs] = b


def _mixer_tail(w, x, gate, ya, yb):
    wa = ya.shape[-1]
    mix = _dot(ya.astype(BF16), w["w_out"][0:wa, :]) + _dot(yb.astype(BF16), w["w_out"][wa:, :])
    return x + gate * mix


def _ffn(w, x, shift, scale, gate, final_norm):
    h = (_rms(x, w["g_ffn"][...]) * (1.0 + scale) + shift).astype(BF16)
    d_ff = w["w_up"].shape[-1]
    ff = None
    for j in range(d_ff // FF_CHUNK):
        cols = slice(j * FF_CHUNK, (j + 1) * FF_CHUNK)
        act = jnp.square(jnp.maximum(_dot(h, w["w_up"][:, cols]), 0.0)).astype(BF16)
        part = _dot(act, w["w_down"][cols, :])
        ff = part if ff is None else ff + part
    out = x + gate * ff
    if final_norm:
        out = _rms(out, w["g_final"][...])
    return out


_WEIGHT_NAMES = ("g_mix", "w_in", "b_in", "conv_a_w", "conv_a_b", "ln_g", "ln_b", "conv_b_w", "conv_b_b",
                 "w_gate", "b_ga", "b_gx", "lam", "w_out", "g_ffn", "w_up", "w_down", "g_final")


def _prompt_body(final_norm, dims, x_ref, mod_ref, *rest):
    ka, kb, ha, hb = dims
    nw = len(_WEIGHT_NAMES)
    w = dict(zip(_WEIGHT_NAMES, rest[:nw]))
    xo_ref, na_ref, nb_ref, nh_ref, gbuf, rbuf, cabuf, cbbuf, abuf, bbuf, hcar = rest[nw:]
    n_rows = x_ref.shape[0]
    wa = gbuf.shape[-1]
    wb = rbuf.shape[-1]
    t = pl.program_id(1)

    @pl.when(t == 0)
    def _():
        gbuf[0:ha, :] = jnp.zeros((ha, wa), F32)
        rbuf[0:hb, :] = jnp.zeros((hb, wb), F32)
        hcar[...] = jnp.zeros_like(hcar)

    x = x_ref[...]
    proj = _mixer_head(x, mod_ref[0:1, :], mod_ref[1:2, :], w)
    gbuf[ha:ha + n_rows, :] = proj(0, wa) * _sigmoid(proj(wa, 2 * wa))
    rbuf[hb:hb + n_rows, :] = proj(2 * wa, 2 * wa + wb)
    _conv_rows(gbuf, w["conv_a_w"], cabuf, n_rows, ka, ha)
    _conv_rows(rbuf, w["conv_b_w"], cbbuf, n_rows, kb, hb)
    ya = _silu(_layer_norm(cabuf[...] + w["conv_a_b"][...], w["ln_g"][...], w["ln_b"][...]))
    first_row = (lax.broadcasted_iota(jnp.int32, (n_rows, 1), 0) + t * n_rows) == 0
    _mixer_gates(w, cbbuf, abuf, bbuf, first_row)
    _scan_rows(abuf, bbuf, hcar, n_rows)
    yb = bbuf[...] * _gelu_tanh(proj(2 * wa + wb, 2 * wa + 2 * wb))
    x1 = _mixer_tail(w, x, mod_ref[2:3, :], ya, yb)
    xo_ref[...] = _ffn(w, x1, mod_ref[3:4, :], mod_ref[4:5, :], mod_ref[5:6, :], final_norm)

    @pl.when(t == pl.num_programs(1) - 1)
    def _():
        na_ref[...] = gbuf[n_rows + ha - (ka - 1):n_rows + ha, :]
        nb_ref[...] = rbuf[n_rows + hb - (kb - 1):n_rows + hb, :]
        nh_ref[...] = hcar[0:1, :]

    gbuf[0:ha, :] = gbuf[n_rows:n_rows + ha, :]
    rbuf[0:hb, :] = rbuf[n_rows:n_rows + hb, :]


def _sample_body(final_norm, dims, x_ref, mod_ref, sa_ref, sb_ref, sh_ref, *rest):
    ka, kb, ha, hb = dims
    nw = len(_WEIGHT_NAMES)
    w = dict(zip(_WEIGHT_NAMES, rest[:nw]))
    xo_ref, na_ref, nb_ref, nh_ref, gabuf, urbuf, cabuf, cbbuf, abuf, bbuf = rest[nw:]
    n_rows, d = x_ref.shape
    n_seq = sa_ref.shape[0]
    steps = n_rows // n_seq
    wa = gabuf.shape[-1]
    wb = urbuf.shape[-1]

    def per_token(i):
        return jnp.broadcast_to(mod_ref[i][:, None, :], (n_seq, steps, d)).reshape(n_rows, d)

    x = x_ref[...]
    proj = _mixer_head(x, per_token(0), per_token(1), w)
    gabuf[...] = proj(0, wa) * _sigmoid(proj(wa, 2 * wa))
    urbuf[...] = proj(2 * wa, 2 * wa + wb)
    row = lax.broadcasted_iota(jnp.int32, (SUBLANES, LANES), 0)

    def conv_seq(b, carry):
        rows = pl.ds(pl.multiple_of(b * SUBLANES, SUBLANES), SUBLANES)
        for g in range(wa // LANES):
            lanes = slice(g * LANES, (g + 1) * LANES)

            def tile_a(i):
                if i < ha // SUBLANES:
                    return sa_ref[b, i * SUBLANES:(i + 1) * SUBLANES, lanes]
                return gabuf[rows, lanes]

            cabuf[rows, lanes], _ = _conv_tile(tile_a, w["conv_a_w"], lanes, ka, ha, None, row)
        for g in range(wb // LANES):
            lanes = slice(g * LANES, (g + 1) * LANES)

            def tile_b(i):
                if i < hb // SUBLANES:
                    return sb_ref[b, i * SUBLANES:(i + 1) * SUBLANES, lanes]
                return urbuf[rows, lanes]

            cbbuf[rows, lanes], _ = _conv_tile(tile_b, w["conv_b_w"], lanes, kb, hb, None, row)
        na_ref[b, 0:ka - 1 - steps, :] = sa_ref[b, ha - (ka - 1) + steps:ha, :]
        na_ref[b, ka - 1 - steps:ka - 1, :] = gabuf[rows, :]
        nb_ref[b] = urbuf[rows, :][steps - (kb - 1):steps, :]
        return carry

    lax.fori_loop(0, n_seq, conv_seq, 0)
    ya = _silu(_layer_norm(cabuf[...] + w["conv_a_b"][...], w["ln_g"][...], w["ln_b"][...]))
    _mixer_gates(w, cbbuf, abuf, bbuf, None)
    row_b = lax.broadcasted_iota(jnp.int32, (SUBLANES, wb), 0)

    def scan_seq(b, carry):
        rows = pl.ds(pl.multiple_of(b * SUBLANES, SUBLANES), SUBLANES)
        a_cum, b_cum = _scan_tile(abuf[rows, :], bbuf[rows, :], row_b)
        h = a_cum * sh_ref[b] + b_cum
        bbuf[rows, :] = h
        nh_ref[b] = h[SUBLANES - 1:SUBLANES, :]
        return carry

    lax.fori_loop(0, n_seq, scan_seq, 0, unroll=2)
    yb = bbuf[...] * _gelu_tanh(proj(2 * wa + wb, 2 * wa + 2 * wb))
    x1 = _mixer_tail(w, x, per_token(2), ya, yb)
    xo_ref[...] = _ffn(w, x1, per_token(3), per_token(4), per_token(5), final_norm)


def _ada_body(c_ref, w_ref, b_ref, o_ref):
    c = c_ref[...]
    o_ref[...] = _dot(_silu(c).astype(BF16), w_ref[...].astype(BF16)) + b_ref[...]


def _ada_call(c_all, w_ada, b_ada):
    depth, d, n = w_ada.shape
    rows = c_all.shape[0]
    return pl.pallas_call(
        _ada_body,
        grid=(depth, n // ADA_COLS),
        in_specs=[
            pl.BlockSpec((rows, d), lambda l, j: (0, 0)),
            pl.BlockSpec((None, d, ADA_COLS), lambda l, j: (l, 0, j)),
            pl.BlockSpec((None, 1, ADA_COLS), lambda l, j: (l, 0, j)),
        ],
        out_specs=pl.BlockSpec((None, rows, ADA_COLS), lambda l, j: (l, 0, j)),
        out_shape=jax.ShapeDtypeStruct((depth, rows, n), F32),
        compiler_params=pltpu.CompilerParams(dimension_semantics=("arbitrary", "arbitrary")),
        name="ada_mod",
    )(c_all, w_ada, b_ada.reshape(depth, 1, n))


def _resident(arr, n_grid):
    zeros = (0,) * arr.ndim
    index_map = (lambda a, b: zeros) if n_grid == 2 else (lambda a: zeros)
    return pl.BlockSpec(arr.shape, index_map, pipeline_mode=pl.Buffered(1))


def _prompt_call(x, mod, weights, dims, final_norm):
    batch, seq, d = x.shape
    ka, kb, ha, hb = dims
    wa = weights["conv_a_b"].shape[-1]
    wb = weights["conv_b_b"].shape[-1]
    r = PROMPT_ROWS
    ws = [weights[k] for k in _WEIGHT_NAMES]
    return pl.pallas_call(
        functools.partial(_prompt_body, final_norm, dims),
        grid=(batch, seq // r),
        in_specs=[
            pl.BlockSpec((None, r, d), lambda b, t: (b, t, 0)),
            pl.BlockSpec((None, 6, d), lambda b, t: (b, 0, 0)),
        ] + [_resident(a, 2) for a in ws],
        out_specs=[
            pl.BlockSpec((None, r, d), lambda b, t: (b, t, 0)),
            pl.BlockSpec((None, ka - 1, wa), lambda b, t: (b, 0, 0)),
            pl.BlockSpec((None, kb - 1, wb), lambda b, t: (b, 0, 0)),
            pl.BlockSpec((None, 1, wb), lambda b, t: (b, 0, 0)),
        ],
        out_shape=[
            jax.ShapeDtypeStruct((batch, seq, d), F32),
            jax.ShapeDtypeStruct((batch, ka - 1, wa), F32),
            jax.ShapeDtypeStruct((batch, kb - 1, wb), F32),
            jax.ShapeDtypeStruct((batch, 1, wb), F32),
        ],
        scratch_shapes=[
            pltpu.VMEM((ha + r, wa), F32),
            pltpu.VMEM((hb + r, wb), F32),
            pltpu.VMEM((r, wa), F32),
            pltpu.VMEM((r, wb), F32),
            pltpu.VMEM((r, wb), F32),
            pltpu.VMEM((r, wb), F32),
            pltpu.VMEM((SUBLANES, wb), F32),
        ],
        compiler_params=pltpu.CompilerParams(
            dimension_semantics=("arbitrary", "arbitrary"), vmem_limit_bytes=VMEM_LIMIT),
        name="prompt_layer",
    )(x, mod, *ws)


def _sample_call(x, mod, state_a, state_b, state_h, weights, dims, final_norm):
    n_rows, d = x.shape
    n_seq = state_a.shape[0]
    steps = n_rows // n_seq
    ka, kb, ha, hb = dims
    wa = weights["conv_a_b"].shape[-1]
    wb = weights["conv_b_b"].shape[-1]
    bt = SAMPLE_SEQS
    r = bt * steps
    ws = [weights[k] for k in _WEIGHT_NAMES]
    return pl.pallas_call(
        functools.partial(_sample_body, final_norm, dims),
        grid=(n_seq // bt,),
        in_specs=[
            pl.BlockSpec((r, d), lambda i: (i, 0)),
            pl.BlockSpec((6, bt, d), lambda i: (0, i, 0)),
            pl.BlockSpec((bt, ha, wa), lambda i: (i, 0, 0)),
            pl.BlockSpec((bt, hb, wb), lambda i: (i, 0, 0)),
            pl.BlockSpec((bt, 1, wb), lambda i: (i, 0, 0)),
        ] + [_resident(a, 1) for a in ws],
        out_specs=[
            pl.BlockSpec((r, d), lambda i: (i, 0)),
            pl.BlockSpec((bt, ka - 1, wa), lambda i: (i, 0, 0)),
            pl.BlockSpec((bt, kb - 1, wb), lambda i: (i, 0, 0)),
            pl.BlockSpec((bt, 1, wb), lambda i: (i, 0, 0)),
        ],
        out_shape=[
            jax.ShapeDtypeStruct((n_rows, d), F32),
            jax.ShapeDtypeStruct((n_seq, ka - 1, wa), F32),
            jax.ShapeDtypeStruct((n_seq, kb - 1, wb), F32),
            jax.ShapeDtypeStruct((n_seq, 1, wb), F32),
        ],
        scratch_shapes=[pltpu.VMEM((r, wa), F32), pltpu.VMEM((r, wb), F32)] * 3,
        compiler_params=pltpu.CompilerParams(
            dimension_semantics=("arbitrary",), vmem_limit_bytes=VMEM_LIMIT),
        name="sample_layer",
    )(x, mod, state_a, state_b, state_h, *ws)


def _hist_rows(width):
    return -(-(width - 1) // SUBLANES) * SUBLANES


def _pack_gates(w_ga, w_gx):
    heads, hd, _ = w_ga.shape
    per_block = MXU_DIM // hd
    blocks = []
    for kb in range(heads // per_block):
        blk = jnp.zeros((MXU_DIM, 2 * MXU_DIM), F32)
        for i in range(per_block):
            h = kb * per_block + i
            blk = blk.at[i * hd:(i + 1) * hd, i * hd:(i + 1) * hd].set(w_ga[h])
            blk = blk.at[i * hd:(i + 1) * hd, MXU_DIM + i * hd:MXU_DIM + (i + 1) * hd].set(w_gx[h])
        blocks.append(blk)
    return jnp.stack(blocks).astype(BF16)


def _layer_weights(l, g_mix, w_in, b_in, conv_a_w, conv_a_b, ln_a_g, ln_a_b, conv_b_w, conv_b_b,
                   w_gate_a, b_gate_a, w_gate_x, b_gate_x, lru_lambda, w_out, g_ffn, w_up, w_down, g_final):
    def row(v):
        return v.reshape(1, -1)

    def taps(wk):
        return jnp.broadcast_to(wk[:, None, :], (wk.shape[0], SUBLANES, wk.shape[1]))

    return {
        "g_mix": row(g_mix[l]), "w_in": w_in[l].astype(BF16), "b_in": row(b_in[l]),
        "conv_a_w": taps(conv_a_w[l]), "conv_a_b": row(conv_a_b[l]),
        "ln_g": row(ln_a_g[l]), "ln_b": row(ln_a_b[l]),
        "conv_b_w": taps(conv_b_w[l]), "conv_b_b": row(conv_b_b[l]),
        "w_gate": _pack_gates(w_gate_a[l], w_gate_x[l]),
        "b_ga": row(b_gate_a[l]), "b_gx": row(b_gate_x[l]), "lam": row(lru_lambda[l]),
        "w_out": w_out[l].astype(BF16), "g_ffn": row(g_ffn[l]),
        "w_up": w_up[l].astype(BF16), "w_down": w_down[l].astype(BF16), "g_final": row(g_final),
    }


def kernel(x_prompt, x_sample, c_prompt, c_sample, state_conv_a, state_conv_b, state_lru_h, w_ada, b_ada, g_mix, w_in, b_in, conv_a_w, conv_a_b, ln_a_g, ln_a_b, conv_b_w, conv_b_b, w_gate_a, b_gate_a, w_gate_x, b_gate_x, lru_lambda, w_out, g_ffn, w_up, w_down, g_final):
    batch, seq, d = x_prompt.shape
    n_seq, steps, _ = x_sample.shape
    depth = w_ada.shape[0]
    ka, wa = conv_a_w.shape[1:]
    kb, wb = conv_b_w.shape[1:]
    ha, hb = _hist_rows(ka), _hist_rows(kb)
    dims = (ka, kb, ha, hb)
    assert steps == SUBLANES and seq % PROMPT_ROWS == 0 and n_seq % SAMPLE_SEQS == 0
    assert wa % MXU_DIM == 0 and wb % MXU_DIM == 0 and MXU_DIM % w_gate_a.shape[-1] == 0

    mod = _ada_call(jnp.concatenate([c_prompt, c_sample], axis=0), w_ada, b_ada)
    mod_p = mod[:, :batch].reshape(depth, batch, 6, d)
    mod_s = mod[:, batch:].reshape(depth, n_seq, 6, d).transpose(0, 2, 1, 3)
    sa = jnp.pad(state_conv_a, ((0, 0), (0, 0), (ha - (ka - 1), 0), (0, 0)))
    sb = jnp.pad(state_conv_b, ((0, 0), (0, 0), (hb - (kb - 1), 0), (0, 0)))
    sh = state_lru_h.reshape(depth, n_seq, 1, wb)

    xp = x_prompt
    xs = x_sample.reshape(n_seq * steps, d)
    outs_p, outs_s = [], []
    for l in range(depth):
        weights = _layer_weights(l, g_mix, w_in, b_in, conv_a_w, conv_a_b, ln_a_g, ln_a_b, conv_b_w, conv_b_b,
                                 w_gate_a, b_gate_a, w_gate_x, b_gate_x, lru_lambda, w_out, g_ffn, w_up, w_down,
                                 g_final)
        final_norm = l == depth - 1
        xp, pa, pb, ph = _prompt_call(xp, mod_p[l], weights, dims, final_norm)
        xs, qa, qb, qh = _sample_call(xs, mod_s[l], sa[l], sb[l], sh[l], weights, dims, final_norm)
        outs_p.append((pa, pb, ph.reshape(batch, wb)))
        outs_s.append((qa, qb, qh.reshape(n_seq, wb)))

    def stack(outs, i):
        return jnp.stack([o[i] for o in outs])

    return (xp, xs.reshape(n_seq, steps, d),
            stack(outs_p, 0), stack(outs_p, 1), stack(outs_p, 2),
            stack(outs_s, 0), stack(outs_s, 1), stack(outs_s, 2))
```

```python
import functools

import jax
import jax.numpy as jnp
from jax import lax
from jax.experimental import pallas as pl
from jax.experimental.pallas import tpu as pltpu

F32 = jnp.float32
BF16 = jnp.bfloat16

SUBLANES = 8
LANES = 128
MXU_DIM = 256

RG_C = 8.0
EPS = 1e-6
GELU_C = 0.7978845608028654

PROMPT_ROWS = 256
SAMPLE_SEQS = 32
FF_CHUNK = 512
ADA_COLS = 1024
VMEM_LIMIT = 52 * 1024 * 1024


def _sigmoid(x):
    return 1.0 / (1.0 + jnp.exp(-x))


def _silu(x):
    return x * _sigmoid(x)


def _gelu_tanh(x):
    return x * (0.5 * (1.0 + jnp.tanh(GELU_C * (x + 0.044715 * (x * x * x)))))


def _softplus(x):
    return jnp.maximum(x, 0.0) + jnp.log1p(jnp.exp(-jnp.abs(x)))


def _rms(x, g):
    return x * lax.rsqrt(jnp.mean(x * x, axis=-1, keepdims=True) + EPS) * g


def _layer_norm(x, g, b):
    mu = jnp.mean(x, axis=-1, keepdims=True)
    xc = x - mu
    var = jnp.mean(xc * xc, axis=-1, keepdims=True)
    return xc * lax.rsqrt(var + EPS) * g + b


def _dot(a, b):
    return jnp.dot(a, b, preferred_element_type=F32)


def _pack_rows(w):
    k, n = w.shape[-2:]
    pairs = w.astype(BF16).reshape(w.shape[:-2] + (k // 2, 2, n))
    return lax.bitcast_convert_type(jnp.swapaxes(pairs, -1, -2), jnp.uint32)


def _wmat(ref, rows=slice(None), cols=slice(None), lead=()):
    start = None if rows.start is None else rows.start // 2
    stop = None if rows.stop is None else rows.stop // 2
    return pltpu.bitcast(ref[lead + (slice(start, stop), cols)], BF16)


def _conv_taps(width, hist_rows):
    lead = hist_rows - (width - 1)
    taps = {}
    for k in range(width):
        taps.setdefault((k + lead) % SUBLANES, []).append((k + lead) // SUBLANES)
    return lead, taps


def _conv_tile(tile_fn, w_ref, lanes, width, hist_rows, prev, row):
    lead, taps = _conv_taps(width, hist_rows)

    def z(r, off):
        acc = None
        for q in taps[r]:
            term = w_ref[SUBLANES * q + r - lead, :, lanes] * tile_fn(q + off)
            acc = term if acc is None else acc + term
        return acc

    shifted = [r for r in sorted(taps) if r != 0]
    new = [z(r, 1) for r in shifted]
    if prev is None:
        prev = [z(r, 0) for r in shifted]
    out = z(0, 0)
    for r, p, n in zip(shifted, prev, new):
        out = out + pltpu.roll(jnp.where(row >= r, p, n), SUBLANES - r, axis=0)
    return out, new


def _conv_rows(buf, w_ref, out_ref, n_rows, width, hist_rows, lane_groups):
    row = lax.broadcasted_iota(jnp.int32, (SUBLANES, LANES), 0)
    for g in lane_groups:
        lanes = slice(g * LANES, (g + 1) * LANES)
        prev = None
        for j in range(n_rows // SUBLANES):
            def tile_fn(i, j=j, lanes=lanes):
                return buf[(j + i) * SUBLANES:(j + i + 1) * SUBLANES, lanes]

            out, prev = _conv_tile(tile_fn, w_ref, lanes, width, hist_rows, prev, row)
            out_ref[j * SUBLANES:(j + 1) * SUBLANES, lanes] = out


def _scan_tile(a, b, row):
    for s in (1, 2, 4):
        keep = row >= s
        a_prev = jnp.where(keep, pltpu.roll(a, s, axis=0), 1.0)
        b_prev = jnp.where(keep, pltpu.roll(b, s, axis=0), 0.0)
        b = a * b_prev + b
        a = a * a_prev
    return a, b


def _scan_rows(a_ref, b_ref, carry_ref, n_rows):
    channels = a_ref.shape[-1]
    row = lax.broadcasted_iota(jnp.int32, (SUBLANES, channels), 0)
    h_prev = carry_ref[...]
    for j in range(n_rows // SUBLANES):
        rows = slice(j * SUBLANES, (j + 1) * SUBLANES)
        a_cum, b_cum = _scan_tile(a_ref[rows, :], b_ref[rows, :], row)
        h = a_cum * h_prev + b_cum
        b_ref[rows, :] = h
        h_prev = jnp.broadcast_to(h[SUBLANES - 1:SUBLANES, :], (SUBLANES, channels))
    carry_ref[...] = h_prev


def _lru_inputs(cb, g, b_ga, b_gx, softplus_neg_lam, first_row):
    n = cb.shape[-1]
    gate_a = g[:, 0:n] + b_ga
    gate_x = g[:, n:2 * n] + b_gx
    log_a = -RG_C * _sigmoid(gate_a) * softplus_neg_lam
    a = jnp.exp(log_a)
    mult = jnp.sqrt(-jnp.tanh(log_a) * (a * a + 1.0))
    if first_row is not None:
        mult = jnp.where(first_row, 1.0, mult)
    return a, mult * _sigmoid(gate_x) * cb


def _mixer_head(w, x, shift, scale):
    return (_rms(x, w["g_mix"][...]) * (1.0 + scale) + shift).astype(BF16)


def _proj(w, h, c0, c1):
    return _dot(h, _wmat(w["w_in"], cols=slice(c0, c1))) + w["b_in"][:, c0:c1]


def _mixer_gates(w, cb_ref, a_ref, b_ref, first_row):
    cb = cb_ref[...] + w["conv_b_b"][...]
    cb_lo = cb.astype(BF16)
    sp = _softplus(-w["lam"][...])
    for kb in range(cb.shape[-1] // MXU_DIM):
        cs = slice(kb * MXU_DIM, (kb + 1) * MXU_DIM)
        g = _dot(cb_lo[:, cs], _wmat(w["w_gate"], lead=(kb,)))
        a, b = _lru_inputs(cb[:, cs], g, w["b_ga"][:, cs], w["b_gx"][:, cs], sp[:, cs], first_row)
        a_ref[:, cs] = a
        b_ref[:, cs] = b


def _mixer_tail(w, x, gate, ya, yb):
    wa = ya.shape[-1]
    wy = yb.shape[-1]
    mix = (_dot(ya.astype(BF16), _wmat(w["w_out"], rows=slice(0, wa)))
           + _dot(yb.astype(BF16), _wmat(w["w_out"], rows=slice(wa, wa + wy))))
    return x + gate * mix


def _ffn_head(w, x, shift, scale):
    return (_rms(x, w["g_ffn"][...]) * (1.0 + scale) + shift).astype(BF16)


def _ffn_chunks(w, h, acc_ref, first, last):
    for j in range(first, last):
        cols = slice(j * FF_CHUNK, (j + 1) * FF_CHUNK)
        act = jnp.square(jnp.maximum(_dot(h, _wmat(w["w_up"], cols=cols)), 0.0)).astype(BF16)
        part = _dot(act, _wmat(w["w_down"], rows=cols))
        if j == 0:
            acc_ref[...] = part
        else:
            acc_ref[...] += part


def _ffn_tail(w, x, gate, acc_ref, final_norm):
    out = x + gate * acc_ref[...]
    if final_norm:
        out = _rms(out, w["g_final"][...])
    return out


_WEIGHT_NAMES = ("g_mix", "w_in", "b_in", "conv_a_w", "conv_a_b", "ln_g", "ln_b", "conv_b_w", "conv_b_b",
                 "w_gate", "b_ga", "b_gx", "lam", "w_out", "g_ffn", "w_up", "w_down", "g_final")


def _prompt_body(final_norm, dims, chunks_per_seq, n_chunks, x_ref, mod_ref, mod_ffn_ref, *rest):
    ka, kb, ha, hb = dims
    nw = len(_WEIGHT_NAMES)
    w = dict(zip(_WEIGHT_NAMES, rest[:nw]))
    (xo_ref, na_ref, nb_ref, nh_ref, gbuf, rbuf, cabuf, cbbuf, abuf, bbuf, hcar,
     hbuf, hfbuf, yabuf, gybuf, x1_new, x1_old, ffacc) = rest[nw:]
    n_rows = x_ref.shape[0]
    wa = gbuf.shape[-1]
    wb = rbuf.shape[-1]
    s = pl.program_id(0)
    t = jnp.minimum(s, n_chunks - 1) % chunks_per_seq
    n_ff =w["w_up"].shape[-1] // FF_CHUNK
    groups_a = wa // LANES
    assert groups_a == wb // LANES and n_ff >= groups_a + 3

    @pl.when(s == 0)
    def _():
        x1_old[...] = jnp.zeros_like(x1_old)

    @pl.when(t == 0)
    def _():
        gbuf[0:ha, :] = jnp.zeros((ha, wa), F32)
        rbuf[0:hb, :] = jnp.zeros((hb, wb), F32)
        hcar[...] = jnp.zeros_like(hcar)

    h = _mixer_head(w, x_ref[...], mod_ref[0:1, :], mod_ref[1:2, :])
    hbuf[...] = h
    gbuf[ha:ha + n_rows, :] = _proj(w, h, 0, wa) * _sigmoid(_proj(w, h, wa, 2 * wa))
    rbuf[hb:hb + n_rows, :] = _proj(w, h, 2 * wa, 2 * wa + wb)
    hfbuf[...] = _ffn_head(w, x1_old[...], mod_ffn_ref[3:4, :], mod_ffn_ref[4:5, :])
    for g in range(groups_a):
        _ffn_chunks(w, hfbuf[...], ffacc, g, g + 1)
        _conv_rows(gbuf, w["conv_a_w"], cabuf, n_rows, ka, ha, (g,))
        _conv_rows(rbuf, w["conv_b_w"], cbbuf, n_rows, kb, hb, (g,))
    ya = _silu(_layer_norm(cabuf[...] + w["conv_a_b"][...], w["ln_g"][...], w["ln_b"][...]))
    yabuf[...] = ya.astype(BF16)
    first_row = (lax.broadcasted_iota(jnp.int32, (n_rows, 1), 0) + t * n_rows) == 0
    _mixer_gates(w, cbbuf, abuf, bbuf, first_row)
    gybuf[...] = _gelu_tanh(_proj(w, hbuf[...], 2 * wa + wb, 2 * wa + 2 * wb))
    _ffn_chunks(w, hfbuf[...], ffacc, groups_a, groups_a + 1)
    _scan_rows(abuf, bbuf, hcar, n_rows)
    _ffn_chunks(w, hfbuf[...], ffacc, groups_a + 1, groups_a + 2)
    yb = bbuf[...] * gybuf[...]
    x1_new[...] = _mixer_tail(w, x_ref[...], mod_ref[2:3, :], yabuf[...], yb)
    _ffn_chunks(w, hfbuf[...], ffacc, groups_a + 2, n_ff)
    xo_ref[...] = _ffn_tail(w, x1_old[...], mod_ffn_ref[5:6, :], ffacc, final_norm)
    x1_old[...] = x1_new[...]
    gbuf[0:ha, :] = gbuf[n_rows:n_rows + ha, :]
    rbuf[0:hb, :] = rbuf[n_rows:n_rows + hb, :]

    @pl.when((t == chunks_per_seq - 1) & (s < n_chunks))
    def _():
        na_ref[...] = gbuf[ha - (ka - 1):ha, :]
        nb_ref[...] = rbuf[hb - (kb - 1):hb, :]
        nh_ref[...] = hcar[0:1, :]


def _sample_body(final_norm, dims, x_ref, mod_ref, sa_ref, sb_ref, sh_ref, *rest):
    ka, kb, ha, hb = dims
    nw = len(_WEIGHT_NAMES)
    w = dict(zip(_WEIGHT_NAMES, rest[:nw]))
    xo_ref, na_ref, nb_ref, nh_ref, gabuf, urbuf, cabuf, cbbuf, abuf, bbuf, ffacc = rest[nw:]
    n_rows, d = x_ref.shape
    n_seq = sa_ref.shape[0]
    steps = n_rows // n_seq
    wa = gabuf.shape[-1]
    wb = urbuf.shape[-1]

    def per_token(i):
        return jnp.broadcast_to(mod_ref[i][:, None, :], (n_seq, steps, d)).reshape(n_rows, d)

    x = x_ref[...]
    h = _mixer_head(w, x, per_token(0), per_token(1))
    gabuf[...] = _proj(w, h, 0, wa) * _sigmoid(_proj(w, h, wa, 2 * wa))
    urbuf[...] = _proj(w, h, 2 * wa, 2 * wa + wb)
    row = lax.broadcasted_iota(jnp.int32, (SUBLANES, LANES), 0)

    def conv_seq(b, carry):
        rows = pl.ds(pl.multiple_of(b * SUBLANES, SUBLANES), SUBLANES)
        for g in range(wa // LANES):
            lanes = slice(g * LANES, (g + 1) * LANES)

            def tile_a(i):
                if i < ha // SUBLANES:
                    return sa_ref[b, i * SUBLANES:(i + 1) * SUBLANES, lanes]
                return gabuf[rows, lanes]

            cabuf[rows, lanes], _ = _conv_tile(tile_a, w["conv_a_w"], lanes, ka, ha, None, row)
        for g in range(wb // LANES):
            lanes = slice(g * LANES, (g + 1) * LANES)

            def tile_b(i):
                if i < hb // SUBLANES:
                    return sb_ref[b, i * SUBLANES:(i + 1) * SUBLANES, lanes]
                return urbuf[rows, lanes]

            cbbuf[rows, lanes], _ = _conv_tile(tile_b, w["conv_b_w"], lanes, kb, hb, None, row)
        na_ref[b, 0:ka - 1 - steps, :] = sa_ref[b, ha - (ka - 1) + steps:ha, :]
        na_ref[b, ka - 1 - steps:ka - 1, :] = gabuf[rows, :]
        nb_ref[b] = urbuf[rows, :][steps - (kb - 1):steps, :]
        return carry

    lax.fori_loop(0, n_seq, conv_seq, 0)
    ya = _silu(_layer_norm(cabuf[...] + w["conv_a_b"][...], w["ln_g"][...], w["ln_b"][...]))
    _mixer_gates(w, cbbuf, abuf, bbuf, None)
    row_b = lax.broadcasted_iota(jnp.int32, (SUBLANES, wb), 0)

    def scan_seq(b, carry):
        rows = pl.ds(pl.multiple_of(b * SUBLANES, SUBLANES), SUBLANES)
        a_cum, b_cum = _scan_tile(abuf[rows, :], bbuf[rows, :], row_b)
        h = a_cum * sh_ref[b] + b_cum
        bbuf[rows, :] = h
        nh_ref[b] = h[SUBLANES - 1:SUBLANES, :]
        return carry

    lax.fori_loop(0, n_seq, scan_seq, 0, unroll=2)
    yb = bbuf[...] * _gelu_tanh(_proj(w, h, 2 * wa + wb, 2 * wa + 2 * wb))
    x1 = _mixer_tail(w, x, per_token(2), ya, yb)
    h_ffn = _ffn_head(w, x1, per_token(3), per_token(4))
    _ffn_chunks(w, h_ffn, ffacc, 0, w["w_up"].shape[-1] // FF_CHUNK)
    xo_ref[...] = _ffn_tail(w, x1, per_token(5), ffacc, final_norm)


def _ada_body(c_ref, w_ref, b_ref, o_ref):
    c = c_ref[...]
    o_ref[...] = _dot(_silu(c).astype(BF16), w_ref[...].astype(BF16)) + b_ref[...]


def _ada_call(c_all, w_ada, b_ada):
    depth, d, n = w_ada.shape
    rows = c_all.shape[0]
    return pl.pallas_call(
        _ada_body,
        grid=(depth, n // ADA_COLS),
        in_specs=[
            pl.BlockSpec((rows, d), lambda l, j: (0, 0)),
            pl.BlockSpec((None, d, ADA_COLS), lambda l, j: (l, 0, j)),
            pl.BlockSpec((None, 1, ADA_COLS), lambda l, j: (l, 0, j)),
        ],
        out_specs=pl.BlockSpec((None, rows, ADA_COLS), lambda l, j: (l, 0, j)),
        out_shape=jax.ShapeDtypeStruct((depth, rows, n), F32),
        compiler_params=pltpu.CompilerParams(dimension_semantics=("arbitrary", "arbitrary")),
        name="ada_mod",
    )(c_all, w_ada, b_ada.reshape(depth, 1, n))


def _resident(arr):
    zeros = (0,) * arr.ndim
    return pl.BlockSpec(arr.shape, lambda s: zeros, pipeline_mode=pl.Buffered(1))


def _prompt_call(x, mod, weights, dims, final_norm):
    batch, seq, d = x.shape
    ka, kb, ha, hb = dims
    wa = weights["conv_a_b"].shape[-1]
    wb = weights["conv_b_b"].shape[-1]
    r = PROMPT_ROWS
    per_seq = seq // r
    n_chunks = batch * per_seq
    ws = [weights[k] for k in _WEIGHT_NAMES]

    def mixer_chunk(s):
        return jnp.minimum(s, n_chunks - 1)

    def ffn_chunk(s):
        return jnp.maximum(s - 1, 0)

    return pl.pallas_call(
        functools.partial(_prompt_body, final_norm, dims, per_seq, n_chunks),
        grid=(n_chunks + 1,),
        in_specs=[
            pl.BlockSpec((None, r, d), lambda s: (mixer_chunk(s) // per_seq, mixer_chunk(s) % per_seq, 0)),
            pl.BlockSpec((None, 6, d), lambda s: (mixer_chunk(s) // per_seq, 0, 0)),
            pl.BlockSpec((None, 6, d), lambda s: (ffn_chunk(s) // per_seq, 0, 0)),
        ] + [_resident(a) for a in ws],
        out_specs=[
            pl.BlockSpec((None, r, d), lambda s: (ffn_chunk(s) // per_seq, ffn_chunk(s) % per_seq, 0)),
            pl.BlockSpec((None, ka - 1, wa), lambda s: (mixer_chunk(s) // per_seq, 0, 0)),
            pl.BlockSpec((None, kb - 1, wb), lambda s: (mixer_chunk(s) // per_seq, 0, 0)),
            pl.BlockSpec((None, 1, wb), lambda s: (mixer_chunk(s) // per_seq, 0, 0)),
        ],
        out_shape=[
            jax.ShapeDtypeStruct((batch, seq, d), F32),
            jax.ShapeDtypeStruct((batch, ka - 1, wa), F32),
            jax.ShapeDtypeStruct((batch, kb - 1, wb), F32),
            jax.ShapeDtypeStruct((batch, 1, wb), F32),
        ],
        scratch_shapes=[
            pltpu.VMEM((ha + r, wa), F32),
            pltpu.VMEM((hb + r, wb), F32),
            pltpu.VMEM((r, wa), F32),
            pltpu.VMEM((r, wb), F32),
            pltpu.VMEM((r, wb), F32),
            pltpu.VMEM((r, wb), F32),
            pltpu.VMEM((SUBLANES, wb), F32),
            pltpu.VMEM((r, d), BF16),
            pltpu.VMEM((r, d), BF16),
            pltpu.VMEM((r, wa), BF16),
            pltpu.VMEM((r, wb), F32),
            pltpu.VMEM((r, d), F32),
            pltpu.VMEM((r, d), F32),
            pltpu.VMEM((r, d), F32),
        ],
        compiler_params=pltpu.CompilerParams(
            dimension_semantics=("arbitrary",), vmem_limit_bytes=VMEM_LIMIT),
        name="prompt_layer",
    )(x, mod, mod, *ws)


def _sample_call(x, mod, state_a, state_b, state_h, weights, dims, final_norm):
    n_rows, d = x.shape
    n_seq = state_a.shape[0]
    steps = n_rows // n_seq
    ka, kb, ha, hb = dims
    wa = weights["conv_a_b"].shape[-1]
    wb = weights["conv_b_b"].shape[-1]
    bt = SAMPLE_SEQS
    r = bt * steps
    ws = [weights[k] for k in _WEIGHT_NAMES]
    return pl.pallas_call(
        functools.partial(_sample_body, final_norm, dims),
        grid=(n_seq // bt,),
        in_specs=[
            pl.BlockSpec((r, d), lambda i: (i, 0)),
            pl.BlockSpec((6, bt, d), lambda i: (0, i, 0)),
            pl.BlockSpec((bt, ha, wa), lambda i: (i, 0, 0)),
            pl.BlockSpec((bt, hb, wb), lambda i: (i, 0, 0)),
            pl.BlockSpec((bt, 1, wb), lambda i: (i, 0, 0)),
        ] + [_resident(a) for a in ws],
        out_specs=[
            pl.BlockSpec((r, d), lambda i: (i, 0)),
            pl.BlockSpec((bt, ka - 1, wa), lambda i: (i, 0, 0)),
            pl.BlockSpec((bt, kb - 1, wb), lambda i: (i, 0, 0)),
            pl.BlockSpec((bt, 1, wb), lambda i: (i, 0, 0)),
        ],
        out_shape=[
            jax.ShapeDtypeStruct((n_rows, d), F32),
            jax.ShapeDtypeStruct((n_seq, ka - 1, wa), F32),
            jax.ShapeDtypeStruct((n_seq, kb - 1, wb), F32),
            jax.ShapeDtypeStruct((n_seq, 1, wb), F32),
        ],
        scratch_shapes=[pltpu.VMEM((r, wa), F32), pltpu.VMEM((r, wb), F32)] * 3 + [pltpu.VMEM((r, d), F32)],
        compiler_params=pltpu.CompilerParams(
            dimension_semantics=("arbitrary",), vmem_limit_bytes=VMEM_LIMIT),
        name="sample_layer",
    )(x, mod, state_a, state_b, state_h, *ws)


def _hist_rows(width):
    return -(-(width - 1) // SUBLANES) * SUBLANES


def _pack_gates(w_ga, w_gx):
    heads, hd, _ = w_ga.shape
    per_block = MXU_DIM // hd
    blocks = []
    for kb in range(heads // per_block):
        blk = jnp.zeros((MXU_DIM, 2 * MXU_DIM), F32)
        for i in range(per_block):
            h = kb * per_block + i
            blk = blk.at[i * hd:(i + 1) * hd, i * hd:(i + 1) * hd].set(w_ga[h])
            blk = blk.at[i * hd:(i + 1) * hd, MXU_DIM + i * hd:MXU_DIM + (i + 1) * hd].set(w_gx[h])
        blocks.append(blk)
    return _pack_rows(jnp.stack(blocks))


def _layer_weights(l, g_mix, w_in, b_in, conv_a_w, conv_a_b, ln_a_g, ln_a_b, conv_b_w, conv_b_b,
                   w_gate_a, b_gate_a, w_gate_x, b_gate_x, lru_lambda, w_out, g_ffn, w_up, w_down, g_final):
    def row(v):
        return v.reshape(1, -1)

    def taps(wk):
        return jnp.broadcast_to(wk[:, None, :], (wk.shape[0], SUBLANES, wk.shape[1]))

    return {
        "g_mix": row(g_mix[l]), "w_in": _pack_rows(w_in[l]), "b_in": row(b_in[l]),
        "conv_a_w": taps(conv_a_w[l]), "conv_a_b": row(conv_a_b[l]),
        "ln_g": row(ln_a_g[l]), "ln_b": row(ln_a_b[l]),
        "conv_b_w": taps(conv_b_w[l]), "conv_b_b": row(conv_b_b[l]),
        "w_gate": _pack_gates(w_gate_a[l], w_gate_x[l]),
        "b_ga": row(b_gate_a[l]), "b_gx": row(b_gate_x[l]), "lam": row(lru_lambda[l]),
        "w_out": _pack_rows(w_out[l]), "g_ffn": row(g_ffn[l]),
        "w_up": _pack_rows(w_up[l]), "w_down": _pack_rows(w_down[l]), "g_final": row(g_final),
    }


def kernel(x_prompt, x_sample, c_prompt, c_sample, state_conv_a, state_conv_b, state_lru_h, w_ada, b_ada, g_mix, w_in, b_in, conv_a_w, conv_a_b, ln_a_g, ln_a_b, conv_b_w, conv_b_b, w_gate_a, b_gate_a, w_gate_x, b_gate_x, lru_lambda, w_out, g_ffn, w_up, w_down, g_final):
    batch, seq, d = x_prompt.shape
    n_seq, steps, _ = x_sample.shape
    depth = w_ada.shape[0]
    ka, wa = conv_a_w.shape[1:]
    kb, wb = conv_b_w.shape[1:]
    ha, hb = _hist_rows(ka), _hist_rows(kb)
    dims = (ka, kb, ha, hb)
    assert steps == SUBLANES and seq % PROMPT_ROWS == 0 and n_seq % SAMPLE_SEQS == 0
    assert wa % MXU_DIM == 0 and wb % MXU_DIM == 0 and MXU_DIM % w_gate_a.shape[-1] == 0

    mod = _ada_call(jnp.concatenate([c_prompt, c_sample], axis=0), w_ada, b_ada)
    mod_p = mod[:, :batch].reshape(depth, batch, 6, d)
    mod_s = mod[:, batch:].reshape(depth, n_seq, 6, d).transpose(0, 2, 1, 3)
    sa = jnp.pad(state_conv_a, ((0, 0), (0, 0), (ha - (ka - 1), 0), (0, 0)))
    sb = jnp.pad(state_conv_b, ((0, 0), (0, 0), (hb - (kb - 1), 0), (0, 0)))
    sh = state_lru_h.reshape(depth, n_seq, 1, wb)

    xp = x_prompt
    xs = x_sample.reshape(n_seq * steps, d)
    outs_p, outs_s = [], []
    for l in range(depth):
        weights = _layer_weights(l, g_mix, w_in, b_in, conv_a_w, conv_a_b, ln_a_g, ln_a_b, conv_b_w, conv_b_b,
                                 w_gate_a, b_gate_a, w_gate_x, b_gate_x, lru_lambda, w_out, g_ffn, w_up, w_down,
                                 g_final)
        final_norm = l == depth - 1
        xp, pa, pb, ph = _prompt_call(xp, mod_p[l], weights, dims, final_norm)
        xs, qa, qb, qh = _sample_call(xs, mod_s[l], sa[l], sb[l], sh[l], weights, dims, final_norm)
        outs_p.append((pa, pb, ph.reshape(batch, wb)))
        outs_s.append((qa, qb, qh.reshape(n_seq, wb)))

    def stack(outs, i):
        return jnp.stack([o[i] for o in outs])

    return (xp, xs.reshape(n_seq, steps, d),
            stack(outs_p, 0), stack(outs_p, 1), stack(outs_p, 2),
            stack(outs_s, 0), stack(outs_s, 1), stack(outs_s, 2))
```

```python
import functools

import jax
import jax.numpy as jnp
from jax import lax
from jax.experimental import pallas as pl
from jax.experimental.pallas import tpu as pltpu

F32 = jnp.float32
BF16 = jnp.bfloat16

SUBLANES = 8
LANES = 128
MXU_DIM = 256

RG_C = 8.0
EPS = 1e-6
GELU_C = 0.7978845608028654

PROMPT_ROWS = 256
SAMPLE_SEQS = 32
FF_CHUNK = 512
ADA_COLS = 1024
PACK_ROWS = 256
VMEM_LIMIT = 52 * 1024 * 1024
N_PHASES = 6


def _sigmoid(x):
    return 1.0 / (1.0 + jnp.exp(-x))


def _silu(x):
    return x * _sigmoid(x)


def _gelu_tanh(x):
    return x * (0.5 * (1.0 + jnp.tanh(GELU_C * (x + 0.044715 * (x * x * x)))))


def _softplus(x):
    return jnp.maximum(x, 0.0) + jnp.log1p(jnp.exp(-jnp.abs(x)))


def _rms(x, g):
    return x * lax.rsqrt(jnp.mean(x * x, axis=-1, keepdims=True) + EPS) * g


def _layer_norm(x, g, b):
    mu = jnp.mean(x, axis=-1, keepdims=True)
    xc = x - mu
    var = jnp.mean(xc * xc, axis=-1, keepdims=True)
    return xc * lax.rsqrt(var + EPS) * g + b


def _dot(a, b):
    return jnp.dot(a, b, preferred_element_type=F32)


def _pack_body(w_ref, o_ref):
    o_ref[...] = pltpu.bitcast(w_ref[...].astype(BF16), jnp.uint32)


def _pack_rows(w):
    k, n = w.shape[-2:]
    flat = w.reshape(-1, n)
    rows = flat.shape[0]
    assert k % 2 == 0 and rows % PACK_ROWS == 0
    packed = pl.pallas_call(
        _pack_body,
        grid=(rows // PACK_ROWS,),
        in_specs=[pl.BlockSpec((PACK_ROWS, n), lambda i: (i, 0))],
        out_specs=pl.BlockSpec((PACK_ROWS // 2, n), lambda i: (i, 0)),
        out_shape=jax.ShapeDtypeStruct((rows // 2, n), jnp.uint32),
        compiler_params=pltpu.CompilerParams(dimension_semantics=("arbitrary",)),
        name="pack_weights",
    )(flat)
    return packed.reshape(w.shape[:-2] + (k // 2, n))


def _wmat(ref, rows=slice(None), cols=slice(None), lead=()):
    start = None if rows.start is None else rows.start // 2
    stop = None if rows.stop is None else rows.stop // 2
    return pltpu.bitcast(ref[lead + (slice(start, stop), cols)], BF16)


def _conv_taps(width, hist_rows):
    lead = hist_rows - (width - 1)
    taps = {}
    for k in range(width):
        taps.setdefault((k + lead) % SUBLANES, []).append((k + lead) // SUBLANES)
    return lead, taps


def _conv_tile(tile_fn, w_ref, lanes, width, hist_rows, prev, row):
    lead, taps = _conv_taps(width, hist_rows)

    def z(r, off):
        acc = None
        for q in taps[r]:
            term = w_ref[SUBLANES * q + r - lead, :, lanes] * tile_fn(q + off)
            acc = term if acc is None else acc + term
        return acc

    shifted = [r for r in sorted(taps) if r != 0]
    new = [z(r, 1) for r in shifted]
    if prev is None:
        prev = [z(r, 0) for r in shifted]
    out = z(0, 0)
    for r, p, n in zip(shifted, prev, new):
        out = out + pltpu.roll(jnp.where(row >= r, p, n), SUBLANES - r, axis=0)
    return out, new


def _conv_steps(buf, w_ref, out_ref, n_rows, width, hist_rows, g):
    row = lax.broadcasted_iota(jnp.int32, (SUBLANES, LANES), 0)
    lanes = slice(g * LANES, (g + 1) * LANES)
    carry = {"prev": None}

    def make(j):
        def step():
            def tile_fn(i):
                return buf[(j + i) * SUBLANES:(j + i + 1) * SUBLANES, lanes]

            out, carry["prev"] = _conv_tile(tile_fn, w_ref, lanes, width, hist_rows, carry["prev"], row)
            out_ref[j * SUBLANES:(j + 1) * SUBLANES, lanes] = out

        return step

    return [make(j) for j in range(n_rows // SUBLANES)]


def _alternate(pieces, steps):
    total = sum(c for _, c in pieces)
    done = 0
    cost = 0
    for thunk, c in pieces:
        upto = len(steps) * (2 * cost + c) // (2 * total)
        for step in steps[done:upto]:
            step()
        done = upto
        thunk()
        cost += c
    for step in steps[done:]:
        step()


def _scan_tile(a, b, row):
    for s in (1, 2, 4):
        keep = row >= s
        a_prev = jnp.where(keep, pltpu.roll(a, s, axis=0), 1.0)
        b_prev = jnp.where(keep, pltpu.roll(b, s, axis=0), 0.0)
        b = a * b_prev + b
        a = a * a_prev
    return a, b


def _scan_steps(a_ref, b_ref, carry_ref, n_rows):
    channels = a_ref.shape[-1]
    row = lax.broadcasted_iota(jnp.int32, (SUBLANES, channels), 0)
    n_tiles = n_rows // SUBLANES
    carry = {}

    def make(j):
        def step():
            rows = slice(j * SUBLANES, (j + 1) * SUBLANES)
            a_cum, b_cum = _scan_tile(a_ref[rows, :], b_ref[rows, :], row)
            h = a_cum * (carry_ref[...] if j == 0 else carry["h"]) + b_cum
            b_ref[rows, :] = h
            carry["h"] = jnp.broadcast_to(h[SUBLANES - 1:SUBLANES, :], (SUBLANES, channels))
            if j == n_tiles - 1:
                carry_ref[...] = carry["h"]

        return step

    return [make(j) for j in range(n_tiles)]


def _lru_inputs(cb, g, b_ga, b_gx, softplus_neg_lam, first_row):
    n = cb.shape[-1]
    gate_a = g[:, 0:n] + b_ga
    gate_x = g[:, n:2 * n] + b_gx
    log_a = -RG_C * _sigmoid(gate_a) * softplus_neg_lam
    a = jnp.exp(log_a)
    mult = jnp.sqrt(-jnp.tanh(log_a) * (a * a + 1.0))
    if first_row is not None:
        mult = jnp.where(first_row, 1.0, mult)
    return a, mult * _sigmoid(gate_x) * cb


def _mixer_head(w, x, shift, scale):
    return (_rms(x, w["g_mix"][...]) * (1.0 + scale) + shift).astype(BF16)


def _proj(w, h, c0, c1):
    return _dot(h, _wmat(w["w_in"], cols=slice(c0, c1))) + w["b_in"][:, c0:c1]


def _mixer_gates(w, cb_ref, a_ref, b_ref, first_row):
    cb = cb_ref[...] + w["conv_b_b"][...]
    cb_lo = cb.astype(BF16)
    sp = _softplus(-w["lam"][...])
    for kb in range(cb.shape[-1] // MXU_DIM):
        cs = slice(kb * MXU_DIM, (kb + 1) * MXU_DIM)
        g = _dot(cb_lo[:, cs], _wmat(w["w_gate"], lead=(kb,)))
        a, b = _lru_inputs(cb[:, cs], g, w["b_ga"][:, cs], w["b_gx"][:, cs], sp[:, cs], first_row)
        a_ref[:, cs] = a
        b_ref[:, cs] = b


def _mixer_tail(w, x, gate, ya, yb):
    wa = ya.shape[-1]
    wy = yb.shape[-1]
    mix = (_dot(ya.astype(BF16), _wmat(w["w_out"], rows=slice(0, wa)))
           + _dot(yb.astype(BF16), _wmat(w["w_out"], rows=slice(wa, wa + wy))))
    return x + gate * mix


def _ffn_head(w, x, shift, scale):
    return (_rms(x, w["g_ffn"][...]) * (1.0 + scale) + shift).astype(BF16)


def _ffn_chunks(w, h, acc_ref, first, last):
    for j in range(first, last):
        cols = slice(j * FF_CHUNK, (j + 1) * FF_CHUNK)
        act = jnp.square(jnp.maximum(_dot(h, _wmat(w["w_up"], cols=cols)), 0.0)).astype(BF16)
        part = _dot(act, _wmat(w["w_down"], rows=cols))
        if j == 0:
            acc_ref[...] = part
        else:
            acc_ref[...] += part


def _ffn_pieces(w, hf_ref, act_ref, acc_ref, j):
    cols = slice(j * FF_CHUNK, (j + 1) * FF_CHUNK)
    half = acc_ref.shape[-1] // 2

    def up():
        act_ref[...] = jnp.square(jnp.maximum(_dot(hf_ref[...], _wmat(w["w_up"], cols=cols)), 0.0)).astype(BF16)

    def down(n):
        def piece():
            c = slice(n * half, (n + 1) * half)
            part = _dot(act_ref[...], _wmat(w["w_down"], rows=cols, cols=c))
            if j == 0:
                acc_ref[:, c] = part
            else:
                acc_ref[:, c] += part

        return piece

    return [(up, 2), (down(0), 1), (down(1), 1)]


def _ffn_tail(w, x, gate, acc_ref, final_norm):
    out = x + gate * acc_ref[...]
    if final_norm:
        out = _rms(out, w["g_final"][...])
    return out


_WEIGHT_NAMES = ("g_mix", "w_in", "b_in", "conv_a_w", "conv_a_b", "ln_g", "ln_b", "conv_b_w", "conv_b_b",
                 "w_gate", "b_ga", "b_gx", "lam", "w_out", "g_ffn", "w_up", "w_down", "g_final")


def _prompt_body(final_norm, dims, chunks_per_seq, n_chunks, go_ref, x_ref, mod_ref, mod_ffn_ref, *rest):
    ka, kb, ha, hb = dims
    nw = len(_WEIGHT_NAMES)
    w = dict(zip(_WEIGHT_NAMES, rest[:nw]))
    (xo_ref, na_ref, nb_ref, nh_ref, gbuf, rbuf, cabuf, cbbuf, abuf, bbuf, hcar, hbuf, hfbuf, yabuf, gybuf,
     gatebuf, x1_new, x1_old, ffacc, act0, act1) = rest[nw:]
    acts = (act0, act1)
    n_rows = x_ref.shape[0]
    wa = gbuf.shape[-1]
    wb = rbuf.shape[-1]
    s = pl.program_id(0)
    t = jnp.minimum(s, n_chunks - 1) % chunks_per_seq
    n_ff = w["w_up"].shape[-1] // FF_CHUNK
    assert n_ff == 8

    def phase(k):
        return pl.when(go_ref[k] != 0)

    def ffn(j):
        return _ffn_pieces(w, hfbuf, acts[j % 2], ffacc, j)

    @pl.when(s == 0)
    def _():
        x1_old[...] = jnp.zeros_like(x1_old)
        hfbuf[...] = jnp.zeros_like(hfbuf)

    @pl.when(t == 0)
    def _():
        gbuf[0:ha, :] = jnp.zeros((ha, wa), F32)
        rbuf[0:hb, :] = jnp.zeros((hb, wb), F32)
        hcar[...] = jnp.zeros_like(hcar)

    @phase(0)
    def _():
        h = _mixer_head(w, x_ref[...], mod_ref[0:1, :], mod_ref[1:2, :])
        hbuf[...] = h
        gbuf[ha:ha + n_rows, :] = _proj(w, h, 0, wa) * _sigmoid(_proj(w, h, wa, 2 * wa))
        rbuf[hb:hb + n_rows, :] = _proj(w, h, 2 * wa, 2 * wa + wb)

    @phase(1)
    def _():
        steps = []
        for g in range(wa // LANES):
            conv_a = _conv_steps(gbuf, w["conv_a_w"], cabuf, n_rows, ka, ha, g)
            conv_b = _conv_steps(rbuf, w["conv_b_w"], cbbuf, n_rows, kb, hb, g)
            steps += [f for pair in zip(conv_a, conv_b) for f in pair]
        f0, f1, f2, f3 = ffn(0), ffn(1), ffn(2), ffn(3)
        order = [f0[0], f1[0], f0[1], f0[2], f2[0], f1[1], f1[2], f3[0], f2[1], f2[2], f3[1], f3[2]]
        _alternate(order, steps)

    @phase(2)
    def _():
        block = 4 * SUBLANES

        def layer_norm_rows(i):
            def step():
                rows = slice(i * block, (i + 1) * block)
                ya = _silu(_layer_norm(cabuf[rows, :] + w["conv_a_b"][...], w["ln_g"][...], w["ln_b"][...]))
                yabuf[rows, :] = ya.astype(BF16)

            return step

        def gates():
            cb = (cbbuf[...] + w["conv_b_b"][...]).astype(BF16)
            for k in range(wb // MXU_DIM):
                cs = slice(k * MXU_DIM, (k + 1) * MXU_DIM)
                gatebuf[:, 2 * k * MXU_DIM:2 * (k + 1) * MXU_DIM] = _dot(cb[:, cs], _wmat(w["w_gate"], lead=(k,)))

        def gelu_gate():
            gybuf[...] = _gelu_tanh(_proj(w, hbuf[...], 2 * wa + wb, 2 * wa + 2 * wb))

        _alternate([(gates, 1), (gelu_gate, 1)] + ffn(4), [layer_norm_rows(i) for i in range(n_rows // block)])

    @phase(3)
    def _():
        block = 4 * SUBLANES
        sp = _softplus(-w["lam"][...])

        def lru_rows(i, k):
            def step():
                rows = slice(i * block, (i + 1) * block)
                cs = slice(k * MXU_DIM, (k + 1) * MXU_DIM)
                first_row = (lax.broadcasted_iota(jnp.int32, (block, 1), 0) + (t * n_rows + i * block)) == 0
                cb = cbbuf[rows, cs] + w["conv_b_b"][:, cs]
                g = gatebuf[rows, 2 * k * MXU_DIM:2 * (k + 1) * MXU_DIM]
                a, b = _lru_inputs(cb, g, w["b_ga"][:, cs], w["b_gx"][:, cs], sp[:, cs], first_row)
                abuf[rows, cs] = a
                bbuf[rows, cs] = b

            return step

        steps = [lru_rows(i, k) for i in range(n_rows // block) for k in range(wb // MXU_DIM)]
        _alternate(ffn(5), steps)

    @phase(4)
    def _():
        _alternate(ffn(6), _scan_steps(abuf, bbuf, hcar, n_rows))

    @phase(5)
    def _():
        yb = bbuf[...] * gybuf[...]
        x1_new[...] = _mixer_tail(w, x_ref[...], mod_ref[2:3, :], yabuf[...], yb)
        _alternate(ffn(7), [])
        xo_ref[...] = _ffn_tail(w, x1_old[...], mod_ffn_ref[5:6, :], ffacc, final_norm)
        x1_old[...] = x1_new[...]
        hfbuf[...] = _ffn_head(w, x1_new[...], mod_ref[3:4, :], mod_ref[4:5, :])
        gbuf[0:ha, :] = gbuf[n_rows:n_rows + ha, :]
        rbuf[0:hb, :] = rbuf[n_rows:n_rows + hb, :]

    @pl.when((t == chunks_per_seq - 1) & (s < n_chunks))
    def _():
        na_ref[...] = gbuf[ha - (ka - 1):ha, :]
        nb_ref[...] = rbuf[hb - (kb - 1):hb, :]
        nh_ref[...] = hcar[0:1, :]


def _sample_body(final_norm, dims, x_ref, mod_ref, sa_ref, sb_ref, sh_ref, *rest):
    ka, kb, ha, hb = dims
    nw = len(_WEIGHT_NAMES)
    w = dict(zip(_WEIGHT_NAMES, rest[:nw]))
    xo_ref, na_ref, nb_ref, nh_ref, gabuf, urbuf, cabuf, cbbuf, abuf, bbuf, ffacc = rest[nw:]
    n_rows, d = x_ref.shape
    n_seq = sa_ref.shape[0]
    steps = n_rows // n_seq
    wa = gabuf.shape[-1]
    wb = urbuf.shape[-1]

    def per_token(i):
        return jnp.broadcast_to(mod_ref[i][:, None, :], (n_seq, steps, d)).reshape(n_rows, d)

    x = x_ref[...]
    h = _mixer_head(w, x, per_token(0), per_token(1))
    gabuf[...] = _proj(w, h, 0, wa) * _sigmoid(_proj(w, h, wa, 2 * wa))
    urbuf[...] = _proj(w, h, 2 * wa, 2 * wa + wb)
    row = lax.broadcasted_iota(jnp.int32, (SUBLANES, LANES), 0)

    def conv_seq(b, carry):
        rows = pl.ds(pl.multiple_of(b * SUBLANES, SUBLANES), SUBLANES)
        for g in range(wa // LANES):
            lanes = slice(g * LANES, (g + 1) * LANES)

            def tile_a(i):
                if i < ha // SUBLANES:
                    return sa_ref[b, i * SUBLANES:(i + 1) * SUBLANES, lanes]
                return gabuf[rows, lanes]

            cabuf[rows, lanes], _ = _conv_tile(tile_a, w["conv_a_w"], lanes, ka, ha, None, row)
        for g in range(wb // LANES):
            lanes = slice(g * LANES, (g + 1) * LANES)

            def tile_b(i):
                if i < hb // SUBLANES:
                    return sb_ref[b, i * SUBLANES:(i + 1) * SUBLANES, lanes]
                return urbuf[rows, lanes]

            cbbuf[rows, lanes], _ = _conv_tile(tile_b, w["conv_b_w"], lanes, kb, hb, None, row)
        na_ref[b, 0:ka - 1 - steps, :] = sa_ref[b, ha - (ka - 1) + steps:ha, :]
        na_ref[b, ka - 1 - steps:ka - 1, :] = gabuf[rows, :]
        nb_ref[b] = urbuf[rows, :][steps - (kb - 1):steps, :]
        return carry

    lax.fori_loop(0, n_seq, conv_seq, 0)
    ya = _silu(_layer_norm(cabuf[...] + w["conv_a_b"][...], w["ln_g"][...], w["ln_b"][...]))
    _mixer_gates(w, cbbuf, abuf, bbuf, None)
    row_b = lax.broadcasted_iota(jnp.int32, (SUBLANES, wb), 0)

    def scan_seq(b, carry):
        rows = pl.ds(pl.multiple_of(b * SUBLANES, SUBLANES), SUBLANES)
        a_cum, b_cum = _scan_tile(abuf[rows, :], bbuf[rows, :], row_b)
        h = a_cum * sh_ref[b] + b_cum
        bbuf[rows, :] = h
        nh_ref[b] = h[SUBLANES - 1:SUBLANES, :]
        return carry

    lax.fori_loop(0, n_seq, scan_seq, 0, unroll=2)
    yb = bbuf[...] * _gelu_tanh(_proj(w, h, 2 * wa + wb, 2 * wa + 2 * wb))
    x1 = _mixer_tail(w, x, per_token(2), ya, yb)
    h_ffn = _ffn_head(w, x1, per_token(3), per_token(4))
    _ffn_chunks(w, h_ffn, ffacc, 0, w["w_up"].shape[-1] // FF_CHUNK)
    xo_ref[...] = _ffn_tail(w, x1, per_token(5), ffacc, final_norm)


def _ada_body(c_ref, w_ref, b_ref, o_ref):
    c = c_ref[...]
    o_ref[...] = _dot(_silu(c).astype(BF16), w_ref[...].astype(BF16)) + b_ref[...]


def _ada_call(c_all, w_ada, b_ada):
    depth, d, n = w_ada.shape
    rows = c_all.shape[0]
    return pl.pallas_call(
        _ada_body,
        grid=(depth, n // ADA_COLS),
        in_specs=[
            pl.BlockSpec((rows, d), lambda l, j: (0, 0)),
            pl.BlockSpec((None, d, ADA_COLS), lambda l, j: (l, 0, j)),
            pl.BlockSpec((None, 1, ADA_COLS), lambda l, j: (l, 0, j)),
        ],
        out_specs=pl.BlockSpec((None, rows, ADA_COLS), lambda l, j: (l, 0, j)),
        out_shape=jax.ShapeDtypeStruct((depth, rows, n), F32),
        compiler_params=pltpu.CompilerParams(dimension_semantics=("arbitrary", "arbitrary")),
        name="ada_mod",
    )(c_all, w_ada, b_ada.reshape(depth, 1, n))


def _resident(arr, layer):
    index = (layer,) + (0,) * (arr.ndim - 1)
    return pl.BlockSpec((None,) + arr.shape[1:], lambda s: index, pipeline_mode=pl.Buffered(1))


def _prompt_call(x, mod, weights, layer, dims, final_norm):
    batch, seq, d = x.shape
    ka, kb, ha, hb = dims
    wa = weights["conv_a_b"].shape[-1]
    wb = weights["conv_b_b"].shape[-1]
    r = PROMPT_ROWS
    per_seq = seq // r
    n_chunks = batch * per_seq
    ws = [weights[k] for k in _WEIGHT_NAMES]

    def mixer_chunk(s):
        return jnp.minimum(s, n_chunks - 1)

    def ffn_chunk(s):
        return jnp.maximum(s - 1, 0)

    return pl.pallas_call(
        functools.partial(_prompt_body, final_norm, dims, per_seq, n_chunks),
        grid=(n_chunks + 1,),
        in_specs=[
            pl.BlockSpec(memory_space=pltpu.SMEM),
            pl.BlockSpec((None, r, d), lambda s: (mixer_chunk(s) // per_seq, mixer_chunk(s) % per_seq, 0)),
            pl.BlockSpec((None, 6, d), lambda s: (mixer_chunk(s) // per_seq, 0, 0)),
            pl.BlockSpec((None, 6, d), lambda s: (ffn_chunk(s) // per_seq, 0, 0)),
        ] + [_resident(a, layer) for a in ws],
        out_specs=[
            pl.BlockSpec((None, r, d), lambda s: (ffn_chunk(s) // per_seq, ffn_chunk(s) % per_seq, 0)),
            pl.BlockSpec((None, ka - 1, wa), lambda s: (mixer_chunk(s) // per_seq, 0, 0)),
            pl.BlockSpec((None, kb - 1, wb), lambda s: (mixer_chunk(s) // per_seq, 0, 0)),
            pl.BlockSpec((None, 1, wb), lambda s: (mixer_chunk(s) // per_seq, 0, 0)),
        ],
        out_shape=[
            jax.ShapeDtypeStruct((batch, seq, d), F32),
            jax.ShapeDtypeStruct((batch, ka - 1, wa), F32),
            jax.ShapeDtypeStruct((batch, kb - 1, wb), F32),
            jax.ShapeDtypeStruct((batch, 1, wb), F32),
        ],
        scratch_shapes=[
            pltpu.VMEM((ha + r, wa), F32),
            pltpu.VMEM((hb + r, wb), F32),
            pltpu.VMEM((r, wa), F32),
            pltpu.VMEM((r, wb), F32),
            pltpu.VMEM((r, wb), F32),
            pltpu.VMEM((r, wb), F32),
            pltpu.VMEM((SUBLANES, wb), F32),
            pltpu.VMEM((r, d), BF16),
            pltpu.VMEM((r, d), BF16),
            pltpu.VMEM((r, wa), BF16),
            pltpu.VMEM((r, wb), F32),
            pltpu.VMEM((r, 2 * wb), F32),
            pltpu.VMEM((r, d), F32),
            pltpu.VMEM((r, d), F32),
            pltpu.VMEM((r, d), F32),
            pltpu.VMEM((r, FF_CHUNK), BF16),
            pltpu.VMEM((r, FF_CHUNK), BF16),
        ],
        compiler_params=pltpu.CompilerParams(
            dimension_semantics=("arbitrary",), vmem_limit_bytes=VMEM_LIMIT),
        name="prompt_layer",
    )(jnp.ones((N_PHASES,), jnp.int32), x, mod, mod, *ws)


def _sample_call(x, mod, state_a, state_b, state_h, weights, layer, dims, final_norm):
    n_rows, d = x.shape
    n_seq = state_a.shape[0]
    steps = n_rows // n_seq
    ka, kb, ha, hb = dims
    wa = weights["conv_a_b"].shape[-1]
    wb = weights["conv_b_b"].shape[-1]
    bt = SAMPLE_SEQS
    r = bt * steps
    ws = [weights[k] for k in _WEIGHT_NAMES]
    return pl.pallas_call(
        functools.partial(_sample_body, final_norm, dims),
        grid=(n_seq // bt,),
        in_specs=[
            pl.BlockSpec((r, d), lambda i: (i, 0)),
            pl.BlockSpec((6, bt, d), lambda i: (0, i, 0)),
            pl.BlockSpec((bt, ha, wa), lambda i: (i, 0, 0)),
            pl.BlockSpec((bt, hb, wb), lambda i: (i, 0, 0)),
            pl.BlockSpec((bt, 1, wb), lambda i: (i, 0, 0)),
        ] + [_resident(a, layer) for a in ws],
        out_specs=[
            pl.BlockSpec((r, d), lambda i: (i, 0)),
            pl.BlockSpec((bt, ka - 1, wa), lambda i: (i, 0, 0)),
            pl.BlockSpec((bt, kb - 1, wb), lambda i: (i, 0, 0)),
            pl.BlockSpec((bt, 1, wb), lambda i: (i, 0, 0)),
        ],
        out_shape=[
            jax.ShapeDtypeStruct((n_rows, d), F32),
            jax.ShapeDtypeStruct((n_seq, ka - 1, wa), F32),
            jax.ShapeDtypeStruct((n_seq, kb - 1, wb), F32),
            jax.ShapeDtypeStruct((n_seq, 1, wb), F32),
        ],
        scratch_shapes=[pltpu.VMEM((r, wa), F32), pltpu.VMEM((r, wb), F32)] * 3 + [pltpu.VMEM((r, d), F32)],
        compiler_params=pltpu.CompilerParams(
            dimension_semantics=("arbitrary",), vmem_limit_bytes=VMEM_LIMIT),
        name="sample_layer",
    )(x, mod, state_a, state_b, state_h, *ws)


def _hist_rows(width):
    return -(-(width - 1) // SUBLANES) * SUBLANES


def _gate_blocks(w_ga, w_gx):
    depth, heads, hd, _ = w_ga.shape
    per_block = MXU_DIM // hd
    blocks = jnp.zeros((depth, heads // per_block, MXU_DIM, 2 * MXU_DIM), F32)
    for h in range(heads):
        k, i = divmod(h, per_block)
        rows = slice(i * hd, (i + 1) * hd)
        blocks = blocks.at[:, k, rows, i * hd:(i + 1) * hd].set(w_ga[:, h])
        blocks = blocks.at[:, k, rows, MXU_DIM + i * hd:MXU_DIM + (i + 1) * hd].set(w_gx[:, h])
    return blocks


def _stacked_weights(g_mix, w_in, b_in, conv_a_w, conv_a_b, ln_a_g, ln_a_b, conv_b_w, conv_b_b,
                     w_gate_a, b_gate_a, w_gate_x, b_gate_x, lru_lambda, w_out, g_ffn, w_up, w_down, g_final):
    depth = w_in.shape[0]

    def row(v):
        return v.reshape(depth, 1, -1)

    def taps(wk):
        return jnp.broadcast_to(wk[:, :, None, :], wk.shape[:2] + (SUBLANES, wk.shape[2]))

    return {
        "g_mix": row(g_mix), "w_in": _pack_rows(w_in), "b_in": row(b_in),
        "conv_a_w": taps(conv_a_w), "conv_a_b": row(conv_a_b), "ln_g": row(ln_a_g), "ln_b": row(ln_a_b),
        "conv_b_w": taps(conv_b_w), "conv_b_b": row(conv_b_b),
        "w_gate": _pack_rows(_gate_blocks(w_gate_a, w_gate_x)),
        "b_ga": row(b_gate_a), "b_gx": row(b_gate_x), "lam": row(lru_lambda),
        "w_out": _pack_rows(w_out), "g_ffn": row(g_ffn),
        "w_up": _pack_rows(w_up), "w_down": _pack_rows(w_down),
        "g_final": jnp.broadcast_to(g_final.reshape(1, 1, -1), (depth, 1, g_final.shape[-1])),
    }


def kernel(x_prompt, x_sample, c_prompt, c_sample, state_conv_a, state_conv_b, state_lru_h, w_ada, b_ada, g_mix, w_in, b_in, conv_a_w, conv_a_b, ln_a_g, ln_a_b, conv_b_w, conv_b_b, w_gate_a, b_gate_a, w_gate_x, b_gate_x, lru_lambda, w_out, g_ffn, w_up, w_down, g_final):
    batch, seq, d = x_prompt.shape
    n_seq, steps, _ = x_sample.shape
    depth = w_ada.shape[0]
    ka, wa = conv_a_w.shape[1:]
    kb, wb = conv_b_w.shape[1:]
    ha, hb = _hist_rows(ka), _hist_rows(kb)
    dims = (ka, kb, ha, hb)
    assert steps == SUBLANES and seq % PROMPT_ROWS == 0 and n_seq % SAMPLE_SEQS == 0
    assert wa % MXU_DIM == 0 and wb % MXU_DIM == 0 and MXU_DIM % w_gate_a.shape[-1] == 0

    mod = _ada_call(jnp.concatenate([c_prompt, c_sample], axis=0), w_ada, b_ada)
    mod_p = mod[:, :batch].reshape(depth, batch, 6, d)
    mod_s = mod[:, batch:].reshape(depth, n_seq, 6, d).transpose(0, 2, 1, 3)
    sa = jnp.pad(state_conv_a, ((0, 0), (0, 0), (ha - (ka - 1), 0), (0, 0)))
    sb = jnp.pad(state_conv_b, ((0, 0), (0, 0), (hb - (kb - 1), 0), (0, 0)))
    sh = state_lru_h.reshape(depth, n_seq, 1, wb)

    xp = x_prompt
    xs = x_sample.reshape(n_seq * steps, d)
    outs_p, outs_s = [], []
    weights = _stacked_weights(g_mix, w_in, b_in, conv_a_w, conv_a_b, ln_a_g, ln_a_b, conv_b_w, conv_b_b,
                               w_gate_a, b_gate_a, w_gate_x, b_gate_x, lru_lambda, w_out, g_ffn, w_up, w_down,
                               g_final)
    for l in range(depth):
        final_norm = l == depth - 1
        xp, pa, pb, ph = _prompt_call(xp, mod_p[l], weights, l, dims, final_norm)
        xs, qa, qb, qh = _sample_call(xs, mod_s[l], sa[l], sb[l], sh[l], weights, l, dims, final_norm)
        outs_p.append((pa, pb, ph.reshape(batch, wb)))
        outs_s.append((qa, qb, qh.reshape(n_seq, wb)))

    def stack(outs, i):
        return jnp.stack([o[i] for o in outs])

    return (xp, xs.reshape(n_seq, steps, d),
            stack(outs_p, 0), stack(outs_p, 1), stack(outs_p, 2),
            stack(outs_s, 0), stack(outs_s, 1), stack(outs_s, 2))
```

```python
import functools

import jax
import jax.numpy as jnp
from jax import lax
from jax.experimental import pallas as pl
from jax.experimental.pallas import tpu as pltpu

F32 = jnp.float32
BF16 = jnp.bfloat16

SUBLANES = 8
LANES = 128
MXU_DIM = 256

RG_C = 8.0
EPS = 1e-6
GELU_C = 0.7978845608028654

PROMPT_ROWS = 256
SAMPLE_SEQS = 32
FF_CHUNK = 512
ADA_COLS = 1024
PACK_ROWS = 256
VMEM_LIMIT = 52 * 1024 * 1024


def _sigmoid(x):
    return 1.0 / (1.0 + jnp.exp(-x))


def _silu(x):
    return x * _sigmoid(x)


def _gelu_tanh(x):
    return x * (0.5 * (1.0 + jnp.tanh(GELU_C * (x + 0.044715 * (x * x * x)))))


def _softplus(x):
    return jnp.maximum(x, 0.0) + jnp.log1p(jnp.exp(-jnp.abs(x)))


def _rms(x, g):
    return x * lax.rsqrt(jnp.mean(x * x, axis=-1, keepdims=True) + EPS) * g


def _layer_norm(x, g, b):
    mu = jnp.mean(x, axis=-1, keepdims=True)
    xc = x - mu
    var = jnp.mean(xc * xc, axis=-1, keepdims=True)
    return xc * lax.rsqrt(var + EPS) * g + b


def _dot(a, b):
    return jnp.dot(a, b, preferred_element_type=F32)


def _pack_body(w_ref, o_ref):
    o_ref[...] = pltpu.bitcast(w_ref[...].astype(BF16), jnp.uint32)


def _pack_rows(w):
    k, n = w.shape[-2:]
    flat = w.reshape(-1, n)
    rows = flat.shape[0]
    assert k % 2 == 0 and rows % PACK_ROWS == 0
    packed = pl.pallas_call(
        _pack_body,
        grid=(rows // PACK_ROWS,),
        in_specs=[pl.BlockSpec((PACK_ROWS, n), lambda i: (i, 0))],
        out_specs=pl.BlockSpec((PACK_ROWS // 2, n), lambda i: (i, 0)),
        out_shape=jax.ShapeDtypeStruct((rows // 2, n), jnp.uint32),
        compiler_params=pltpu.CompilerParams(dimension_semantics=("arbitrary",)),
        name="pack_weights",
    )(flat)
    return packed.reshape(w.shape[:-2] + (k // 2, n))


def _wmat(ref, rows=slice(None), cols=slice(None), lead=()):
    start = None if rows.start is None else rows.start // 2
    stop = None if rows.stop is None else rows.stop // 2
    return pltpu.bitcast(ref[lead + (slice(start, stop), cols)], BF16)


def _conv_taps(width, hist_rows):
    lead = hist_rows - (width - 1)
    taps = {}
    for k in range(width):
        taps.setdefault((k + lead) % SUBLANES, []).append((k + lead) // SUBLANES)
    return lead, taps


def _conv_tile(tile_fn, w_ref, lanes, width, hist_rows, prev, row):
    lead, taps = _conv_taps(width, hist_rows)

    def z(r, off):
        acc = None
        for q in taps[r]:
            term = w_ref[SUBLANES * q + r - lead, :, lanes] * tile_fn(q + off)
            acc = term if acc is None else acc + term
        return acc

    shifted = [r for r in sorted(taps) if r != 0]
    new = [z(r, 1) for r in shifted]
    if prev is None:
        prev = [z(r, 0) for r in shifted]
    out = z(0, 0)
    for r, p, n in zip(shifted, prev, new):
        out = out + pltpu.roll(jnp.where(row >= r, p, n), SUBLANES - r, axis=0)
    return out, new


def _conv_rows(buf, w_ref, out_ref, n_rows, width, hist_rows, lane_groups):
    row = lax.broadcasted_iota(jnp.int32, (SUBLANES, LANES), 0)
    for g in lane_groups:
        lanes = slice(g * LANES, (g + 1) * LANES)
        prev = None
        for j in range(n_rows // SUBLANES):
            def tile_fn(i, j=j, lanes=lanes):
                return buf[(j + i) * SUBLANES:(j + i + 1) * SUBLANES, lanes]

            out, prev = _conv_tile(tile_fn, w_ref, lanes, width, hist_rows, prev, row)
            out_ref[j * SUBLANES:(j + 1) * SUBLANES, lanes] = out


def _scan_tile(a, b, row):
    for s in (1, 2, 4):
        keep = row >= s
        a_prev = jnp.where(keep, pltpu.roll(a, s, axis=0), 1.0)
        b_prev = jnp.where(keep, pltpu.roll(b, s, axis=0), 0.0)
        b = a * b_prev + b
        a = a * a_prev
    return a, b


def _scan_rows(a_ref, b_ref, carry_ref, n_rows):
    channels = a_ref.shape[-1]
    row = lax.broadcasted_iota(jnp.int32, (SUBLANES, channels), 0)
    h_prev = carry_ref[...]
    for j in range(n_rows // SUBLANES):
        rows = slice(j * SUBLANES, (j + 1) * SUBLANES)
        a_cum, b_cum = _scan_tile(a_ref[rows, :], b_ref[rows, :], row)
        h = a_cum * h_prev + b_cum
        b_ref[rows, :] = h
        h_prev = jnp.broadcast_to(h[SUBLANES - 1:SUBLANES, :], (SUBLANES, channels))
    carry_ref[...] = h_prev


def _lru_inputs(cb, g, b_ga, b_gx, softplus_neg_lam, first_row):
    n = cb.shape[-1]
    gate_a = g[:, 0:n] + b_ga
    gate_x = g[:, n:2 * n] + b_gx
    log_a = -RG_C * _sigmoid(gate_a) * softplus_neg_lam
    a = jnp.exp(log_a)
    mult = jnp.sqrt(-jnp.tanh(log_a) * (a * a + 1.0))
    if first_row is not None:
        mult = jnp.where(first_row, 1.0, mult)
    return a, mult * _sigmoid(gate_x) * cb


def _mixer_head(w, x, shift, scale):
    return (_rms(x, w["g_mix"][...]) * (1.0 + scale) + shift).astype(BF16)


def _proj(w, h, c0, c1):
    return _dot(h, _wmat(w["w_in"], cols=slice(c0, c1))) + w["b_in"][:, c0:c1]


def _mixer_gates(w, cb_ref, a_ref, b_ref, first_row):
    cb = cb_ref[...] + w["conv_b_b"][...]
    cb_lo = cb.astype(BF16)
    sp = _softplus(-w["lam"][...])
    for kb in range(cb.shape[-1] // MXU_DIM):
        cs = slice(kb * MXU_DIM, (kb + 1) * MXU_DIM)
        g = _dot(cb_lo[:, cs], _wmat(w["w_gate"], lead=(kb,)))
        a, b = _lru_inputs(cb[:, cs], g, w["b_ga"][:, cs], w["b_gx"][:, cs], sp[:, cs], first_row)
        a_ref[:, cs] = a
        b_ref[:, cs] = b


def _mixer_tail(w, x, gate, ya, yb):
    wa = ya.shape[-1]
    wy = yb.shape[-1]
    mix = (_dot(ya.astype(BF16), _wmat(w["w_out"], rows=slice(0, wa)))
           + _dot(yb.astype(BF16), _wmat(w["w_out"], rows=slice(wa, wa + wy))))
    return x + gate * mix


def _ffn_head(w, x, shift, scale):
    return (_rms(x, w["g_ffn"][...]) * (1.0 + scale) + shift).astype(BF16)


def _ffn_chunks(w, h, acc_ref, first, last):
    for j in range(first, last):
        cols = slice(j * FF_CHUNK, (j + 1) * FF_CHUNK)
        act = jnp.square(jnp.maximum(_dot(h, _wmat(w["w_up"], cols=cols)), 0.0)).astype(BF16)
        part = _dot(act, _wmat(w["w_down"], rows=cols))
        if j == 0:
            acc_ref[...] = part
        else:
            acc_ref[...] += part


def _ffn_tail(w, x, gate, acc_ref, final_norm):
    out = x + gate * acc_ref[...]
    if final_norm:
        out = _rms(out, w["g_final"][...])
    return out


_WEIGHT_NAMES = ("g_mix", "w_in", "b_in", "conv_a_w", "conv_a_b", "ln_g", "ln_b", "conv_b_w", "conv_b_b",
                 "w_gate", "b_ga", "b_gx", "lam", "w_out", "g_ffn", "w_up", "w_down", "g_final")


def _prompt_body(final_norm, dims, chunks_per_seq, n_chunks, x_ref, mod_ref, mod_ffn_ref, *rest):
    ka, kb, ha, hb = dims
    nw = len(_WEIGHT_NAMES)
    w = dict(zip(_WEIGHT_NAMES, rest[:nw]))
    (xo_ref, na_ref, nb_ref, nh_ref, gbuf, rbuf, cabuf, cbbuf, abuf, bbuf, hcar,
     hbuf, hfbuf, yabuf, gybuf, x1_new, x1_old, ffacc) = rest[nw:]
    n_rows = x_ref.shape[0]
    wa = gbuf.shape[-1]
    wb = rbuf.shape[-1]
    s = pl.program_id(0)
    t = jnp.minimum(s, n_chunks - 1) % chunks_per_seq
    n_ff = w["w_up"].shape[-1] // FF_CHUNK
    groups_a = wa // LANES
    assert groups_a == wb // LANES and n_ff >= groups_a + 3

    @pl.when(s == 0)
    def _():
        x1_old[...] = jnp.zeros_like(x1_old)

    @pl.when(t == 0)
    def _():
        gbuf[0:ha, :] = jnp.zeros((ha, wa), F32)
        rbuf[0:hb, :] = jnp.zeros((hb, wb), F32)
        hcar[...] = jnp.zeros_like(hcar)

    h = _mixer_head(w, x_ref[...], mod_ref[0:1, :], mod_ref[1:2, :])
    hbuf[...] = h
    gbuf[ha:ha + n_rows, :] = _proj(w, h, 0, wa) * _sigmoid(_proj(w, h, wa, 2 * wa))
    rbuf[hb:hb + n_rows, :] = _proj(w, h, 2 * wa, 2 * wa + wb)
    hfbuf[...] = _ffn_head(w, x1_old[...], mod_ffn_ref[3:4, :], mod_ffn_ref[4:5, :])
    for g in range(groups_a):
        _ffn_chunks(w, hfbuf[...], ffacc, g, g + 1)
        _conv_rows(gbuf, w["conv_a_w"], cabuf, n_rows, ka, ha, (g,))
        _conv_rows(rbuf, w["conv_b_w"], cbbuf, n_rows, kb, hb, (g,))
    ya = _silu(_layer_norm(cabuf[...] + w["conv_a_b"][...], w["ln_g"][...], w["ln_b"][...]))
    yabuf[...] = ya.astype(BF16)
    first_row = (lax.broadcasted_iota(jnp.int32, (n_rows, 1), 0) + t * n_rows) == 0
    _mixer_gates(w, cbbuf, abuf, bbuf, first_row)
    gybuf[...] = _gelu_tanh(_proj(w, hbuf[...], 2 * wa + wb, 2 * wa + 2 * wb))
    _ffn_chunks(w, hfbuf[...], ffacc, groups_a, groups_a + 1)
    _scan_rows(abuf, bbuf, hcar, n_rows)
    _ffn_chunks(w, hfbuf[...], ffacc, groups_a + 1, groups_a + 2)
    yb = bbuf[...] * gybuf[...]
    x1_new[...] = _mixer_tail(w, x_ref[...], mod_ref[2:3, :], yabuf[...], yb)
    _ffn_chunks(w, hfbuf[...], ffacc, groups_a + 2, n_ff)
    xo_ref[...] = _ffn_tail(w, x1_old[...], mod_ffn_ref[5:6, :], ffacc, final_norm)
    x1_old[...] = x1_new[...]
    gbuf[0:ha, :] = gbuf[n_rows:n_rows + ha, :]
    rbuf[0:hb, :] = rbuf[n_rows:n_rows + hb, :]

    @pl.when((t == chunks_per_seq - 1) & (s < n_chunks))
    def _():
        na_ref[...] = gbuf[ha - (ka - 1):ha, :]
        nb_ref[...] = rbuf[hb - (kb - 1):hb, :]
        nh_ref[...] = hcar[0:1, :]


def _sample_body(dims, x_ref, mod_ref, sa_ref, sb_ref, sh_ref, *rest):
    ka, kb, ha, hb = dims
    nw = len(_WEIGHT_NAMES)
    w = dict(zip(_WEIGHT_NAMES, rest[:nw]))
    xo_ref, na_ref, nb_ref, nh_ref, xres, gabuf, urbuf, cabuf, cbbuf, abuf, bbuf, ffacc = rest[nw:]
    layer = pl.program_id(0)
    blk = pl.program_id(1)
    n_rows, d = x_ref.shape
    n_seq = sa_ref.shape[0]
    steps = n_rows // n_seq
    wa = gabuf.shape[-1]
    wb = urbuf.shape[-1]
    lead_a = ha - (ka - 1)

    @pl.when(layer == 0)
    def _():
        xres[blk] = x_ref[...]

    def per_token(i):
        return jnp.broadcast_to(mod_ref[i][:, None, :], (n_seq, steps, d)).reshape(n_rows, d)

    x = xres[blk]
    h = _mixer_head(w, x, per_token(0), per_token(1))
    gabuf[...] = _proj(w, h, 0, wa) * _sigmoid(_proj(w, h, wa, 2 * wa))
    urbuf[...] = _proj(w, h, 2 * wa, 2 * wa + wb)
    row = lax.broadcasted_iota(jnp.int32, (SUBLANES, LANES), 0)

    def conv_seq(b, carry):
        rows = pl.ds(pl.multiple_of(b * SUBLANES, SUBLANES), SUBLANES)
        for g in range(wa // LANES):
            lanes = slice(g * LANES, (g + 1) * LANES)

            def tile_a(i):
                if i == 0:
                    return pltpu.roll(sa_ref[b, 0:SUBLANES, lanes], lead_a, axis=0)
                if i < ha // SUBLANES:
                    return sa_ref[b, i * SUBLANES - lead_a:(i + 1) * SUBLANES - lead_a, lanes]
                return gabuf[rows, lanes]

            cabuf[rows, lanes], _ = _conv_tile(tile_a, w["conv_a_w"], lanes, ka, ha, None, row)
        for g in range(wb // LANES):
            lanes = slice(g * LANES, (g + 1) * LANES)

            def tile_b(i):
                if i < hb // SUBLANES:
                    return sb_ref[b, i * SUBLANES:(i + 1) * SUBLANES, lanes]
                return urbuf[rows, lanes]

            cbbuf[rows, lanes], _ = _conv_tile(tile_b, w["conv_b_w"], lanes, kb, hb, None, row)
        na_ref[b, 0:ka - 1 - steps, :] = sa_ref[b, steps:ka - 1, :]
        na_ref[b, ka - 1 - steps:ka - 1, :] = gabuf[rows, :]
        nb_ref[b] = urbuf[rows, :][steps - (kb - 1):steps, :]
        return carry

    lax.fori_loop(0, n_seq, conv_seq, 0)
    ya = _silu(_layer_norm(cabuf[...] + w["conv_a_b"][...], w["ln_g"][...], w["ln_b"][...]))
    _mixer_gates(w, cbbuf, abuf, bbuf, None)
    row_b = lax.broadcasted_iota(jnp.int32, (SUBLANES, wb), 0)

    def scan_seq(b, carry):
        rows = pl.ds(pl.multiple_of(b * SUBLANES, SUBLANES), SUBLANES)
        a_cum, b_cum = _scan_tile(abuf[rows, :], bbuf[rows, :], row_b)
        hs = a_cum * sh_ref[b] + b_cum
        bbuf[rows, :] = hs
        nh_ref[b] = hs[SUBLANES - 1:SUBLANES, :]
        return carry

    lax.fori_loop(0, n_seq, scan_seq, 0, unroll=2)
    yb = bbuf[...] * _gelu_tanh(_proj(w, h, 2 * wa + wb, 2 * wa + 2 * wb))
    x1 = _mixer_tail(w, x, per_token(2), ya, yb)
    h_ffn = _ffn_head(w, x1, per_token(3), per_token(4))
    _ffn_chunks(w, h_ffn, ffacc, 0, w["w_up"].shape[-1] // FF_CHUNK)
    out = _ffn_tail(w, x1, per_token(5), ffacc, False)
    xres[blk] = out
    xo_ref[...] = jnp.where(layer == pl.num_programs(0) - 1, _rms(out, w["g_final"][...]), out)


def _ada_body(c_ref, w_ref, b_ref, o_ref):
    c = c_ref[...]
    o_ref[...] = _dot(_silu(c).astype(BF16), w_ref[...].astype(BF16)) + b_ref[...]


def _ada_call(c_all, w_ada, b_ada):
    depth, d, n = w_ada.shape
    rows = c_all.shape[0]
    assert ADA_COLS == d
    return pl.pallas_call(
        _ada_body,
        grid=(depth, n // d),
        in_specs=[
            pl.BlockSpec((rows, d), lambda l, j: (0, 0)),
            pl.BlockSpec((None, d, d), lambda l, j: (l, 0, j)),
            pl.BlockSpec((None, 1, d), lambda l, j: (l, 0, j)),
        ],
        out_specs=pl.BlockSpec((None, None, rows, d), lambda l, j: (l, j, 0, 0)),
        out_shape=jax.ShapeDtypeStruct((depth, n // d, rows, d), F32),
        compiler_params=pltpu.CompilerParams(dimension_semantics=("arbitrary", "arbitrary")),
        name="ada_mod",
    )(c_all, w_ada, b_ada.reshape(depth, 1, n))


def _resident(arr, layer=None):
    tail = (0,) * (arr.ndim - 1)
    index_map = (lambda l, i: (l,) + tail) if layer is None else (lambda s: (layer,) + tail)
    return pl.BlockSpec((None,) + arr.shape[1:], index_map, pipeline_mode=pl.Buffered(1))


def _prompt_call(x, mod, weights, layer, dims, final_norm):
    batch, seq, d = x.shape
    ka, kb, ha, hb = dims
    wa = weights["conv_a_b"].shape[-1]
    wb = weights["conv_b_b"].shape[-1]
    r = PROMPT_ROWS
    per_seq = seq // r
    n_chunks = batch * per_seq
    ws = [weights[k] for k in _WEIGHT_NAMES]

    def mixer_chunk(s):
        return jnp.minimum(s, n_chunks - 1)

    def ffn_chunk(s):
        return jnp.maximum(s - 1, 0)

    return pl.pallas_call(
        functools.partial(_prompt_body, final_norm, dims, per_seq, n_chunks),
        grid=(n_chunks + 1,),
        in_specs=[
            pl.BlockSpec((None, r, d), lambda s: (mixer_chunk(s) // per_seq, mixer_chunk(s) % per_seq, 0)),
            pl.BlockSpec((None, 6, d), lambda s: (mixer_chunk(s) // per_seq, 0, 0)),
            pl.BlockSpec((None, 6, d), lambda s: (ffn_chunk(s) // per_seq, 0, 0)),
        ] + [_resident(a, layer) for a in ws],
        out_specs=[
            pl.BlockSpec((None, r, d), lambda s: (ffn_chunk(s) // per_seq, ffn_chunk(s) % per_seq, 0)),
            pl.BlockSpec((None, ka - 1, wa), lambda s: (mixer_chunk(s) // per_seq, 0, 0)),
            pl.BlockSpec((None, kb - 1, wb), lambda s: (mixer_chunk(s) // per_seq, 0, 0)),
            pl.BlockSpec((None, 1, wb), lambda s: (mixer_chunk(s) // per_seq, 0, 0)),
        ],
        out_shape=[
            jax.ShapeDtypeStruct((batch, seq, d), F32),
            jax.ShapeDtypeStruct((batch, ka - 1, wa), F32),
            jax.ShapeDtypeStruct((batch, kb - 1, wb), F32),
            jax.ShapeDtypeStruct((batch, 1, wb), F32),
        ],
        scratch_shapes=[
            pltpu.VMEM((ha + r, wa), F32),
            pltpu.VMEM((hb + r, wb), F32),
            pltpu.VMEM((r, wa), F32),
            pltpu.VMEM((r, wb), F32),
            pltpu.VMEM((r, wb), F32),
            pltpu.VMEM((r, wb), F32),
            pltpu.VMEM((SUBLANES, wb), F32),
            pltpu.VMEM((r, d), BF16),
            pltpu.VMEM((r, d), BF16),
            pltpu.VMEM((r, wa), BF16),
            pltpu.VMEM((r, wb), F32),
            pltpu.VMEM((r, d), F32),
            pltpu.VMEM((r, d), F32),
            pltpu.VMEM((r, d), F32),
        ],
        compiler_params=pltpu.CompilerParams(
            dimension_semantics=("arbitrary",), vmem_limit_bytes=VMEM_LIMIT),
        name="prompt_layer",
    )(x, mod, mod, *ws)


def _sample_call(x, mod, state_a, state_b, state_h, weights, dims):
    n_rows, d = x.shape
    depth, n_seq = state_a.shape[:2]
    steps = n_rows // n_seq
    ka, kb, ha, hb = dims
    wa = weights["conv_a_b"].shape[-1]
    wb = weights["conv_b_b"].shape[-1]
    bt = SAMPLE_SEQS
    r = bt * steps
    ws = [weights[k] for k in _WEIGHT_NAMES]

    def per_block(shape):
        return pl.BlockSpec((None, bt) + shape, lambda l, i: (l, i, 0, 0))

    return pl.pallas_call(
        functools.partial(_sample_body, dims),
        grid=(depth, n_seq // bt),
        in_specs=[
            pl.BlockSpec((r, d), lambda l, i: (i, 0)),
            pl.BlockSpec((None, 6, bt, d), lambda l, i: (l, 0, i, 0)),
            per_block((ka - 1, wa)),
            per_block((hb, wb)),
            per_block((1, wb)),
        ] + [_resident(a) for a in ws],
        out_specs=[
            pl.BlockSpec((None, r, d), lambda l, i: (l, i, 0)),
            per_block((ka - 1, wa)),
            per_block((kb - 1, wb)),
            per_block((1, wb)),
        ],
        out_shape=[
            jax.ShapeDtypeStruct((depth, n_rows, d), F32),
            jax.ShapeDtypeStruct((depth, n_seq, ka - 1, wa), F32),
            jax.ShapeDtypeStruct((depth, n_seq, kb - 1, wb), F32),
            jax.ShapeDtypeStruct((depth, n_seq, 1, wb), F32),
        ],
        scratch_shapes=[pltpu.VMEM((n_seq // bt, r, d), F32)]
        + [pltpu.VMEM((r, wa), F32), pltpu.VMEM((r, wb), F32)] * 3 + [pltpu.VMEM((r, d), F32)],
        compiler_params=pltpu.CompilerParams(
            dimension_semantics=("arbitrary", "arbitrary"), vmem_limit_bytes=VMEM_LIMIT),
        name="sample_layers",
    )(x, mod, state_a, state_b, state_h, *ws)


def _hist_rows(width):
    return -(-(width - 1) // SUBLANES) * SUBLANES


def _gate_blocks(w_ga, w_gx):
    depth, heads, hd, _ = w_ga.shape
    per_block = MXU_DIM // hd
    blocks = jnp.zeros((depth, heads // per_block, MXU_DIM, 2 * MXU_DIM), F32)
    for h in range(heads):
        k, i = divmod(h, per_block)
        rows = slice(i * hd, (i + 1) * hd)
        blocks = blocks.at[:, k, rows, i * hd:(i + 1) * hd].set(w_ga[:, h])
        blocks = blocks.at[:, k, rows, MXU_DIM + i * hd:MXU_DIM + (i + 1) * hd].set(w_gx[:, h])
    return blocks


def _stacked_weights(g_mix, w_in, b_in, conv_a_w, conv_a_b, ln_a_g, ln_a_b, conv_b_w, conv_b_b,
                     w_gate_a, b_gate_a, w_gate_x, b_gate_x, lru_lambda, w_out, g_ffn, w_up, w_down, g_final):
    depth = w_in.shape[0]

    def row(v):
        return v.reshape(depth, 1, -1)

    def taps(wk):
        return jnp.broadcast_to(wk[:, :, None, :], wk.shape[:2] + (SUBLANES, wk.shape[2]))

    return {
        "g_mix": row(g_mix), "w_in": _pack_rows(w_in), "b_in": row(b_in),
        "conv_a_w": taps(conv_a_w), "conv_a_b": row(conv_a_b), "ln_g": row(ln_a_g), "ln_b": row(ln_a_b),
        "conv_b_w": taps(conv_b_w), "conv_b_b": row(conv_b_b),
        "w_gate": _pack_rows(_gate_blocks(w_gate_a, w_gate_x)),
        "b_ga": row(b_gate_a), "b_gx": row(b_gate_x), "lam": row(lru_lambda),
        "w_out": _pack_rows(w_out), "g_ffn": row(g_ffn),
        "w_up": _pack_rows(w_up), "w_down": _pack_rows(w_down),
        "g_final": jnp.broadcast_to(g_final.reshape(1, 1, -1), (depth, 1, g_final.shape[-1])),
    }


def kernel(x_prompt, x_sample, c_prompt, c_sample, state_conv_a, state_conv_b, state_lru_h, w_ada, b_ada, g_mix, w_in, b_in, conv_a_w, conv_a_b, ln_a_g, ln_a_b, conv_b_w, conv_b_b, w_gate_a, b_gate_a, w_gate_x, b_gate_x, lru_lambda, w_out, g_ffn, w_up, w_down, g_final):
    batch, seq, d = x_prompt.shape
    n_seq, steps, _ = x_sample.shape
    depth = w_ada.shape[0]
    ka, wa = conv_a_w.shape[1:]
    kb, wb = conv_b_w.shape[1:]
    ha, hb = _hist_rows(ka), _hist_rows(kb)
    dims = (ka, kb, ha, hb)
    assert steps == SUBLANES and seq % PROMPT_ROWS == 0 and n_seq % SAMPLE_SEQS == 0
    assert wa % MXU_DIM == 0 and wb % MXU_DIM == 0 and MXU_DIM % w_gate_a.shape[-1] == 0

    mod = _ada_call(jnp.concatenate([c_sample, c_prompt], axis=0), w_ada, b_ada)
    mod_p = mod[:, :, n_seq:].transpose(0, 2, 1, 3)
    sb = jnp.pad(state_conv_b, ((0, 0), (0, 0), (hb - (kb - 1), 0), (0, 0)))
    weights = _stacked_weights(g_mix, w_in, b_in, conv_a_w, conv_a_b, ln_a_g, ln_a_b, conv_b_w, conv_b_b,
                               w_gate_a, b_gate_a, w_gate_x, b_gate_x, lru_lambda, w_out, g_ffn, w_up, w_down,
                               g_final)
    xs, qa, qb, qh = _sample_call(x_sample.reshape(n_seq * steps, d), mod, state_conv_a, sb,
                                  state_lru_h.reshape(depth, n_seq, 1, wb), weights, dims)
    xp = x_prompt
    outs_p = []
    for l in range(depth):
        xp, pa, pb, ph = _prompt_call(xp, mod_p[l], weights, l, dims, l == depth - 1)
        outs_p.append((pa, pb, ph.reshape(batch, wb)))

    def stack(outs, i):
        return jnp.stack([o[i] for o in outs])

    return (xp, xs[depth - 1].reshape(n_seq, steps, d),
            stack(outs_p, 0), stack(outs_p, 1), stack(outs_p, 2),
            qa, qb, qh.reshape(depth, n_seq, wb))
```

```python
import functools

import jax
import jax.numpy as jnp
from jax import lax
from jax.experimental import pallas as pl
from jax.experimental.pallas import tpu as pltpu

F32 = jnp.float32
BF16 = jnp.bfloat16

SUBLANES = 8
LANES = 128
MXU_DIM = 256

RG_C = 8.0
EPS = 1e-6
GELU_C = 0.7978845608028654

PROMPT_ROWS = 256
SAMPLE_SEQS = 32
FF_CHUNK = 512
ADA_COLS = 1024
PACK_ROWS = 256
VMEM_LIMIT = 52 * 1024 * 1024


def _sigmoid(x):
    return 1.0 / (1.0 + jnp.exp(-x))


def _silu(x):
    return x * _sigmoid(x)


def _gelu_tanh(x):
    return x * (0.5 * (1.0 + jnp.tanh(GELU_C * (x + 0.044715 * (x * x * x)))))


def _softplus(x):
    return jnp.maximum(x, 0.0) + jnp.log1p(jnp.exp(-jnp.abs(x)))


def _rms(x, g):
    return x * lax.rsqrt(jnp.mean(x * x, axis=-1, keepdims=True) + EPS) * g


def _layer_norm(x, g, b):
    mu = jnp.mean(x, axis=-1, keepdims=True)
    xc = x - mu
    var = jnp.mean(xc * xc, axis=-1, keepdims=True)
    return xc * lax.rsqrt(var + EPS) * g + b


def _dot(a, b):
    return jnp.dot(a, b, preferred_element_type=F32)


def _pack_body(w_ref, o_ref):
    o_ref[...] = pltpu.bitcast(w_ref[...].astype(BF16), jnp.uint32)


def _pack_rows(w):
    k, n = w.shape[-2:]
    flat = w.reshape(-1, n)
    rows = flat.shape[0]
    assert k % 2 == 0 and rows % PACK_ROWS == 0
    packed = pl.pallas_call(
        _pack_body,
        grid=(rows // PACK_ROWS,),
        in_specs=[pl.BlockSpec((PACK_ROWS, n), lambda i: (i, 0))],
        out_specs=pl.BlockSpec((PACK_ROWS // 2, n), lambda i: (i, 0)),
        out_shape=jax.ShapeDtypeStruct((rows // 2, n), jnp.uint32),
        compiler_params=pltpu.CompilerParams(dimension_semantics=("arbitrary",)),
        name="pack_weights",
    )(flat)
    return packed.reshape(w.shape[:-2] + (k // 2, n))


def _wmat(ref, rows=slice(None), cols=slice(None), lead=()):
    start = None if rows.start is None else rows.start // 2
    stop = None if rows.stop is None else rows.stop // 2
    return pltpu.bitcast(ref[lead + (slice(start, stop), cols)], BF16)


def _conv_taps(width, hist_rows):
    lead = hist_rows - (width - 1)
    taps = {}
    for k in range(width):
        taps.setdefault((k + lead) % SUBLANES, []).append((k + lead) // SUBLANES)
    return lead, taps


def _conv_tile(tile_fn, w_ref, lanes, width, hist_rows, prev, row):
    lead, taps = _conv_taps(width, hist_rows)

    def z(r, off):
        acc = None
        for q in taps[r]:
            term = w_ref[SUBLANES * q + r - lead, :, lanes] * tile_fn(q + off)
            acc = term if acc is None else acc + term
        return acc

    shifted = [r for r in sorted(taps) if r != 0]
    new = [z(r, 1) for r in shifted]
    if prev is None:
        prev = [z(r, 0) for r in shifted]
    out = z(0, 0)
    for r, p, n in zip(shifted, prev, new):
        out = out + pltpu.roll(jnp.where(row >= r, p, n), SUBLANES - r, axis=0)
    return out, new


def _conv_steps(buf, w_ref, out_ref, n_rows, width, hist_rows, g):
    row = lax.broadcasted_iota(jnp.int32, (SUBLANES, LANES), 0)
    lanes = slice(g * LANES, (g + 1) * LANES)
    carry = {"prev": None}

    def make(j):
        def step():
            def tile_fn(i):
                return buf[(j + i) * SUBLANES:(j + i + 1) * SUBLANES, lanes]

            out, carry["prev"] = _conv_tile(tile_fn, w_ref, lanes, width, hist_rows, carry["prev"], row)
            out_ref[j * SUBLANES:(j + 1) * SUBLANES, lanes] = out

        return step

    return [make(j) for j in range(n_rows // SUBLANES)]


def _alternate(pieces, steps):
    total = sum(c for _, c in pieces)
    done = 0
    cost = 0
    for thunk, c in pieces:
        upto = len(steps) * (2 * cost + c) // (2 * total)
        for step in steps[done:upto]:
            step()
        done = upto
        thunk()
        cost += c
    for step in steps[done:]:
        step()


def _scan_tile(a, b, row):
    for s in (1, 2, 4):
        keep = row >= s
        a_prev = jnp.where(keep, pltpu.roll(a, s, axis=0), 1.0)
        b_prev = jnp.where(keep, pltpu.roll(b, s, axis=0), 0.0)
        b = a * b_prev + b
        a = a * a_prev
    return a, b


def _scan_rows(a_ref, b_ref, carry_ref, n_rows):
    channels = a_ref.shape[-1]
    row = lax.broadcasted_iota(jnp.int32, (SUBLANES, channels), 0)
    h_prev = carry_ref[...]
    for j in range(n_rows // SUBLANES):
        rows = slice(j * SUBLANES, (j + 1) * SUBLANES)
        a_cum, b_cum = _scan_tile(a_ref[rows, :], b_ref[rows, :], row)
        h = a_cum * h_prev + b_cum
        b_ref[rows, :] = h
        h_prev = jnp.broadcast_to(h[SUBLANES - 1:SUBLANES, :], (SUBLANES, channels))
    carry_ref[...] = h_prev


def _lru_inputs(cb, g, b_ga, b_gx, softplus_neg_lam, first_row):
    n = cb.shape[-1]
    gate_a = g[:, 0:n] + b_ga
    gate_x = g[:, n:2 * n] + b_gx
    log_a = -RG_C * _sigmoid(gate_a) * softplus_neg_lam
    a = jnp.exp(log_a)
    mult = jnp.sqrt(-jnp.tanh(log_a) * (a * a + 1.0))
    if first_row is not None:
        mult = jnp.where(first_row, 1.0, mult)
    return a, mult * _sigmoid(gate_x) * cb


def _mixer_head(w, x, shift, scale):
    return (_rms(x, w["g_mix"][...]) * (1.0 + scale) + shift).astype(BF16)


def _proj(w, h, c0, c1):
    return _dot(h, _wmat(w["w_in"], cols=slice(c0, c1))) + w["b_in"][:, c0:c1]


def _mixer_gates(w, cb_ref, a_ref, b_ref, first_row):
    cb = cb_ref[...] + w["conv_b_b"][...]
    cb_lo = cb.astype(BF16)
    sp = _softplus(-w["lam"][...])
    for kb in range(cb.shape[-1] // MXU_DIM):
        cs = slice(kb * MXU_DIM, (kb + 1) * MXU_DIM)
        g = _dot(cb_lo[:, cs], _wmat(w["w_gate"], lead=(kb,)))
        a, b = _lru_inputs(cb[:, cs], g, w["b_ga"][:, cs], w["b_gx"][:, cs], sp[:, cs], first_row)
        a_ref[:, cs] = a
        b_ref[:, cs] = b


def _mixer_tail(w, x, gate, ya, yb):
    wa = ya.shape[-1]
    wy = yb.shape[-1]
    mix = (_dot(ya.astype(BF16), _wmat(w["w_out"], rows=slice(0, wa)))
           + _dot(yb.astype(BF16), _wmat(w["w_out"], rows=slice(wa, wa + wy))))
    return x + gate * mix


def _ffn_head(w, x, shift, scale):
    return (_rms(x, w["g_ffn"][...]) * (1.0 + scale) + shift).astype(BF16)


def _ffn_chunks(w, h, acc_ref, first, last):
    for j in range(first, last):
        cols = slice(j * FF_CHUNK, (j + 1) * FF_CHUNK)
        act = jnp.square(jnp.maximum(_dot(h, _wmat(w["w_up"], cols=cols)), 0.0)).astype(BF16)
        part = _dot(act, _wmat(w["w_down"], rows=cols))
        if j == 0:
            acc_ref[...] = part
        else:
            acc_ref[...] += part


def _ffn_pieces(w, hf_ref, act_ref, acc_ref, j):
    cols = slice(j * FF_CHUNK, (j + 1) * FF_CHUNK)
    half = acc_ref.shape[-1] // 2

    def up():
        act_ref[...] = jnp.square(jnp.maximum(_dot(hf_ref[...], _wmat(w["w_up"], cols=cols)), 0.0)).astype(BF16)

    def down(n):
        def piece():
            c = slice(n * half, (n + 1) * half)
            part = _dot(act_ref[...], _wmat(w["w_down"], rows=cols, cols=c))
            if j == 0:
                acc_ref[:, c] = part
            else:
                acc_ref[:, c] += part

        return piece

    return [(up, 2), (down(0), 1), (down(1), 1)]


def _ffn_tail(w, x, gate, acc_ref, final_norm):
    out = x + gate * acc_ref[...]
    if final_norm:
        out = _rms(out, w["g_final"][...])
    return out


_WEIGHT_NAMES = ("g_mix", "w_in", "b_in", "conv_a_w", "conv_a_b", "ln_g", "ln_b", "conv_b_w", "conv_b_b",
                 "w_gate", "b_ga", "b_gx", "lam", "w_out", "g_ffn", "w_up", "w_down", "g_final")


def _prompt_body(final_norm, dims, chunks_per_seq, n_chunks, go_ref, x_ref, mod_ref, mod_ffn_ref, *rest):
    ka, kb, ha, hb = dims
    nw = len(_WEIGHT_NAMES)
    w = dict(zip(_WEIGHT_NAMES, rest[:nw]))
    (xo_ref, na_ref, nb_ref, nh_ref, gbuf, rbuf, cabuf, cbbuf, abuf, bbuf, hcar,
     hbuf, hfbuf, yabuf, gybuf, x1_new, x1_old, ffacc, act0, act1) = rest[nw:]
    acts = (act0, act1)
    n_rows = x_ref.shape[0]
    wa = gbuf.shape[-1]
    wb = rbuf.shape[-1]
    s = pl.program_id(0)
    t = jnp.minimum(s, n_chunks - 1) % chunks_per_seq
    n_ff = w["w_up"].shape[-1] // FF_CHUNK
    assert n_ff == 8

    @pl.when(s == 0)
    def _():
        x1_old[...] = jnp.zeros_like(x1_old)

    @pl.when(t == 0)
    def _():
        gbuf[0:ha, :] = jnp.zeros((ha, wa), F32)
        rbuf[0:hb, :] = jnp.zeros((hb, wb), F32)
        hcar[...] = jnp.zeros_like(hcar)

    h = _mixer_head(w, x_ref[...], mod_ref[0:1, :], mod_ref[1:2, :])
    hbuf[...] = h
    gbuf[ha:ha + n_rows, :] = _proj(w, h, 0, wa) * _sigmoid(_proj(w, h, wa, 2 * wa))
    rbuf[hb:hb + n_rows, :] = _proj(w, h, 2 * wa, 2 * wa + wb)
    hfbuf[...] = _ffn_head(w, x1_old[...], mod_ffn_ref[3:4, :], mod_ffn_ref[4:5, :])

    @pl.when(go_ref[0] != 0)
    def _():
        steps = []
        for g in range(wa // LANES):
            conv_a = _conv_steps(gbuf, w["conv_a_w"], cabuf, n_rows, ka, ha, g)
            conv_b = _conv_steps(rbuf, w["conv_b_w"], cbbuf, n_rows, kb, hb, g)
            steps += [f for pair in zip(conv_a, conv_b) for f in pair]
        f0, f1, f2, f3 = (_ffn_pieces(w, hfbuf, acts[j % 2], ffacc, j) for j in range(4))
        order = [f0[0], f1[0], f0[1], f0[2], f2[0], f1[1], f1[2], f3[0], f2[1], f2[2], f3[1], f3[2]]
        _alternate(order, steps)

    ya = _silu(_layer_norm(cabuf[...] + w["conv_a_b"][...], w["ln_g"][...], w["ln_b"][...]))
    yabuf[...] = ya.astype(BF16)
    first_row = (lax.broadcasted_iota(jnp.int32, (n_rows, 1), 0) + t * n_rows) == 0
    _mixer_gates(w, cbbuf, abuf, bbuf, first_row)
    gybuf[...] = _gelu_tanh(_proj(w, hbuf[...], 2 * wa + wb, 2 * wa + 2 * wb))
    _ffn_chunks(w, hfbuf[...], ffacc, 4, 5)
    _scan_rows(abuf, bbuf, hcar, n_rows)
    _ffn_chunks(w, hfbuf[...], ffacc, 5, 6)
    yb = bbuf[...] * gybuf[...]
    x1_new[...] = _mixer_tail(w, x_ref[...], mod_ref[2:3, :], yabuf[...], yb)
    _ffn_chunks(w, hfbuf[...], ffacc, 6, n_ff)
    xo_ref[...] = _ffn_tail(w, x1_old[...], mod_ffn_ref[5:6, :], ffacc, final_norm)
    x1_old[...] = x1_new[...]
    gbuf[0:ha, :] = gbuf[n_rows:n_rows + ha, :]
    rbuf[0:hb, :] = rbuf[n_rows:n_rows + hb, :]

    @pl.when((t == chunks_per_seq - 1) & (s < n_chunks))
    def _():
        na_ref[...] = gbuf[ha - (ka - 1):ha, :]
        nb_ref[...] = rbuf[hb - (kb - 1):hb, :]
        nh_ref[...] = hcar[0:1, :]


def _sample_body(dims, x_ref, mod_ref, sa_ref, sb_ref, sh_ref, *rest):
    ka, kb, ha, hb = dims
    nw = len(_WEIGHT_NAMES)
    w = dict(zip(_WEIGHT_NAMES, rest[:nw]))
    xo_ref, na_ref, nb_ref, nh_ref, xres, gabuf, urbuf, cabuf, cbbuf, abuf, bbuf, ffacc = rest[nw:]
    layer = pl.program_id(0)
    blk = pl.program_id(1)
    n_rows, d = x_ref.shape
    n_seq = sa_ref.shape[0]
    steps = n_rows // n_seq
    wa = gabuf.shape[-1]
    wb = urbuf.shape[-1]
    lead_a = ha - (ka - 1)

    @pl.when(layer == 0)
    def _():
        xres[blk] = x_ref[...]

    def per_token(i):
        return jnp.broadcast_to(mod_ref[i][:, None, :], (n_seq, steps, d)).reshape(n_rows, d)

    x = xres[blk]
    h = _mixer_head(w, x, per_token(0), per_token(1))
    gabuf[...] = _proj(w, h, 0, wa) * _sigmoid(_proj(w, h, wa, 2 * wa))
    urbuf[...] = _proj(w, h, 2 * wa, 2 * wa + wb)
    row = lax.broadcasted_iota(jnp.int32, (SUBLANES, LANES), 0)

    def conv_seq(b, carry):
        rows = pl.ds(pl.multiple_of(b * SUBLANES, SUBLANES), SUBLANES)
        for g in range(wa // LANES):
            lanes = slice(g * LANES, (g + 1) * LANES)

            def tile_a(i):
                if i == 0:
                    return pltpu.roll(sa_ref[b, 0:SUBLANES, lanes], lead_a, axis=0)
                if i < ha // SUBLANES:
                    return sa_ref[b, i * SUBLANES - lead_a:(i + 1) * SUBLANES - lead_a, lanes]
                return gabuf[rows, lanes]

            cabuf[rows, lanes], _ = _conv_tile(tile_a, w["conv_a_w"], lanes, ka, ha, None, row)
        for g in range(wb // LANES):
            lanes = slice(g * LANES, (g + 1) * LANES)

            def tile_b(i):
                if i < hb // SUBLANES:
                    return sb_ref[b, i * SUBLANES:(i + 1) * SUBLANES, lanes]
                return urbuf[rows, lanes]

            cbbuf[rows, lanes], _ = _conv_tile(tile_b, w["conv_b_w"], lanes, kb, hb, None, row)
        na_ref[b, 0:ka - 1 - steps, :] = sa_ref[b, steps:ka - 1, :]
        na_ref[b, ka - 1 - steps:ka - 1, :] = gabuf[rows, :]
        nb_ref[b] = urbuf[rows, :][steps - (kb - 1):steps, :]
        return carry

    lax.fori_loop(0, n_seq, conv_seq, 0)
    ya = _silu(_layer_norm(cabuf[...] + w["conv_a_b"][...], w["ln_g"][...], w["ln_b"][...]))
    _mixer_gates(w, cbbuf, abuf, bbuf, None)
    row_b = lax.broadcasted_iota(jnp.int32, (SUBLANES, wb), 0)

    def scan_seq(b, carry):
        rows = pl.ds(pl.multiple_of(b * SUBLANES, SUBLANES), SUBLANES)
        a_cum, b_cum = _scan_tile(abuf[rows, :], bbuf[rows, :], row_b)
        hs = a_cum * sh_ref[b] + b_cum
        bbuf[rows, :] = hs
        nh_ref[b] = hs[SUBLANES - 1:SUBLANES, :]
        return carry

    lax.fori_loop(0, n_seq, scan_seq, 0, unroll=2)
    yb = bbuf[...] * _gelu_tanh(_proj(w, h, 2 * wa + wb, 2 * wa + 2 * wb))
    x1 = _mixer_tail(w, x, per_token(2), ya, yb)
    h_ffn = _ffn_head(w, x1, per_token(3), per_token(4))
    _ffn_chunks(w, h_ffn, ffacc, 0, w["w_up"].shape[-1] // FF_CHUNK)
    out = _ffn_tail(w, x1, per_token(5), ffacc, False)
    xres[blk] = out
    xo_ref[...] = jnp.where(layer == pl.num_programs(0) - 1, _rms(out, w["g_final"][...]), out)


def _ada_body(c_ref, w_ref, b_ref, o_ref):
    c = c_ref[...]
    o_ref[...] = _dot(_silu(c).astype(BF16), w_ref[...].astype(BF16)) + b_ref[...]


def _ada_call(c_all, w_ada, b_ada):
    depth, d, n = w_ada.shape
    rows = c_all.shape[0]
    assert ADA_COLS == d
    return pl.pallas_call(
        _ada_body,
        grid=(depth, n // d),
        in_specs=[
            pl.BlockSpec((rows, d), lambda l, j: (0, 0)),
            pl.BlockSpec((None, d, d), lambda l, j: (l, 0, j)),
            pl.BlockSpec((None, 1, d), lambda l, j: (l, 0, j)),
        ],
        out_specs=pl.BlockSpec((None, None, rows, d), lambda l, j: (l, j, 0, 0)),
        out_shape=jax.ShapeDtypeStruct((depth, n // d, rows, d), F32),
        compiler_params=pltpu.CompilerParams(dimension_semantics=("arbitrary", "arbitrary")),
        name="ada_mod",
    )(c_all, w_ada, b_ada.reshape(depth, 1, n))


def _resident(arr, layer=None):
    tail = (0,) * (arr.ndim - 1)
    index_map = (lambda l, i: (l,) + tail) if layer is None else (lambda s: (layer,) + tail)
    return pl.BlockSpec((None,) + arr.shape[1:], index_map, pipeline_mode=pl.Buffered(1))


def _prompt_call(x, mod, weights, layer, dims, final_norm):
    batch, seq, d = x.shape
    ka, kb, ha, hb = dims
    wa = weights["conv_a_b"].shape[-1]
    wb = weights["conv_b_b"].shape[-1]
    r = PROMPT_ROWS
    per_seq = seq // r
    n_chunks = batch * per_seq
    ws = [weights[k] for k in _WEIGHT_NAMES]

    def mixer_chunk(s):
        return jnp.minimum(s, n_chunks - 1)

    def ffn_chunk(s):
        return jnp.maximum(s - 1, 0)

    return pl.pallas_call(
        functools.partial(_prompt_body, final_norm, dims, per_seq, n_chunks),
        grid=(n_chunks + 1,),
        in_specs=[
            pl.BlockSpec(memory_space=pltpu.SMEM),
            pl.BlockSpec((None, r, d), lambda s: (mixer_chunk(s) // per_seq, mixer_chunk(s) % per_seq, 0)),
            pl.BlockSpec((None, 6, d), lambda s: (mixer_chunk(s) // per_seq, 0, 0)),
            pl.BlockSpec((None, 6, d), lambda s: (ffn_chunk(s) // per_seq, 0, 0)),
        ] + [_resident(a, layer) for a in ws],
        out_specs=[
            pl.BlockSpec((None, r, d), lambda s: (ffn_chunk(s) // per_seq, ffn_chunk(s) % per_seq, 0)),
            pl.BlockSpec((None, ka - 1, wa), lambda s: (mixer_chunk(s) // per_seq, 0, 0)),
            pl.BlockSpec((None, kb - 1, wb), lambda s: (mixer_chunk(s) // per_seq, 0, 0)),
            pl.BlockSpec((None, 1, wb), lambda s: (mixer_chunk(s) // per_seq, 0, 0)),
        ],
        out_shape=[
            jax.ShapeDtypeStruct((batch, seq, d), F32),
            jax.ShapeDtypeStruct((batch, ka - 1, wa), F32),
            jax.ShapeDtypeStruct((batch, kb - 1, wb), F32),
            jax.ShapeDtypeStruct((batch, 1, wb), F32),
        ],
        scratch_shapes=[
            pltpu.VMEM((ha + r, wa), F32),
            pltpu.VMEM((hb + r, wb), F32),
            pltpu.VMEM((r, wa), F32),
            pltpu.VMEM((r, wb), F32),
            pltpu.VMEM((r, wb), F32),
            pltpu.VMEM((r, wb), F32),
            pltpu.VMEM((SUBLANES, wb), F32),
            pltpu.VMEM((r, d), BF16),
            pltpu.VMEM((r, d), BF16),
            pltpu.VMEM((r, wa), BF16),
            pltpu.VMEM((r, wb), F32),
            pltpu.VMEM((r, d), F32),
            pltpu.VMEM((r, d), F32),
            pltpu.VMEM((r, d), F32),
            pltpu.VMEM((r, FF_CHUNK), BF16),
            pltpu.VMEM((r, FF_CHUNK), BF16),
        ],
        compiler_params=pltpu.CompilerParams(
            dimension_semantics=("arbitrary",), vmem_limit_bytes=VMEM_LIMIT),
        name="prompt_layer",
    )(jnp.ones((1,), jnp.int32), x, mod, mod, *ws)


def _sample_call(x, mod, state_a, state_b, state_h, weights, dims):
    n_rows, d = x.shape
    depth, n_seq = state_a.shape[:2]
    steps = n_rows // n_seq
    ka, kb, ha, hb = dims
    wa = weights["conv_a_b"].shape[-1]
    wb = weights["conv_b_b"].shape[-1]
    bt = SAMPLE_SEQS
    r = bt * steps
    ws = [weights[k] for k in _WEIGHT_NAMES]

    def per_block(shape):
        return pl.BlockSpec((None, bt) + shape, lambda l, i: (l, i, 0, 0))

    return pl.pallas_call(
        functools.partial(_sample_body, dims),
        grid=(depth, n_seq // bt),
        in_specs=[
            pl.BlockSpec((r, d), lambda l, i: (i, 0)),
            pl.BlockSpec((None, 6, bt, d), lambda l, i: (l, 0, i, 0)),
            per_block((ka - 1, wa)),
            per_block((hb, wb)),
            per_block((1, wb)),
        ] + [_resident(a) for a in ws],
        out_specs=[
            pl.BlockSpec((None, r, d), lambda l, i: (l, i, 0)),
            per_block((ka - 1, wa)),
            per_block((kb - 1, wb)),
            per_block((1, wb)),
        ],
        out_shape=[
            jax.ShapeDtypeStruct((depth, n_rows, d), F32),
            jax.ShapeDtypeStruct((depth, n_seq, ka - 1, wa), F32),
            jax.ShapeDtypeStruct((depth, n_seq, kb - 1, wb), F32),
            jax.ShapeDtypeStruct((depth, n_seq, 1, wb), F32),
        ],
        scratch_shapes=[pltpu.VMEM((n_seq // bt, r, d), F32)]
        + [pltpu.VMEM((r, wa), F32), pltpu.VMEM((r, wb), F32)] * 3 + [pltpu.VMEM((r, d), F32)],
        compiler_params=pltpu.CompilerParams(
            dimension_semantics=("arbitrary", "arbitrary"), vmem_limit_bytes=VMEM_LIMIT),
        name="sample_layers",
    )(x, mod, state_a, state_b, state_h, *ws)


def _hist_rows(width):
    return -(-(width - 1) // SUBLANES) * SUBLANES


def _gate_blocks(w_ga, w_gx):
    depth, heads, hd, _ = w_ga.shape
    per_block = MXU_DIM // hd
    blocks = jnp.zeros((depth, heads // per_block, MXU_DIM, 2 * MXU_DIM), F32)
    for h in range(heads):
        k, i = divmod(h, per_block)
        rows = slice(i * hd, (i + 1) * hd)
        blocks = blocks.at[:, k, rows, i * hd:(i + 1) * hd].set(w_ga[:, h])
        blocks = blocks.at[:, k, rows, MXU_DIM + i * hd:MXU_DIM + (i + 1) * hd].set(w_gx[:, h])
    return blocks


def _stacked_weights(g_mix, w_in, b_in, conv_a_w, conv_a_b, ln_a_g, ln_a_b, conv_b_w, conv_b_b,
                     w_gate_a, b_gate_a, w_gate_x, b_gate_x, lru_lambda, w_out, g_ffn, w_up, w_down, g_final):
    depth = w_in.shape[0]

    def row(v):
        return v.reshape(depth, 1, -1)

    def taps(wk):
        return jnp.broadcast_to(wk[:, :, None, :], wk.shape[:2] + (SUBLANES, wk.shape[2]))

    return {
        "g_mix": row(g_mix), "w_in": _pack_rows(w_in), "b_in": row(b_in),
        "conv_a_w": taps(conv_a_w), "conv_a_b": row(conv_a_b), "ln_g": row(ln_a_g), "ln_b": row(ln_a_b),
        "conv_b_w": taps(conv_b_w), "conv_b_b": row(conv_b_b),
        "w_gate": _pack_rows(_gate_blocks(w_gate_a, w_gate_x)),
        "b_ga": row(b_gate_a), "b_gx": row(b_gate_x), "lam": row(lru_lambda),
        "w_out": _pack_rows(w_out), "g_ffn": row(g_ffn),
        "w_up": _pack_rows(w_up), "w_down": _pack_rows(w_down),
        "g_final": jnp.broadcast_to(g_final.reshape(1, 1, -1), (depth, 1, g_final.shape[-1])),
    }


def kernel(x_prompt, x_sample, c_prompt, c_sample, state_conv_a, state_conv_b, state_lru_h, w_ada, b_ada, g_mix, w_in, b_in, conv_a_w, conv_a_b, ln_a_g, ln_a_b, conv_b_w, conv_b_b, w_gate_a, b_gate_a, w_gate_x, b_gate_x, lru_lambda, w_out, g_ffn, w_up, w_down, g_final):
    batch, seq, d = x_prompt.shape
    n_seq, steps, _ = x_sample.shape
    depth = w_ada.shape[0]
    ka, wa = conv_a_w.shape[1:]
    kb, wb = conv_b_w.shape[1:]
    ha, hb = _hist_rows(ka), _hist_rows(kb)
    dims = (ka, kb, ha, hb)
    assert steps == SUBLANES and seq % PROMPT_ROWS == 0 and n_seq % SAMPLE_SEQS == 0
    assert wa % MXU_DIM == 0 and wb % MXU_DIM == 0 and MXU_DIM % w_gate_a.shape[-1] == 0

    mod = _ada_call(jnp.concatenate([c_sample, c_prompt], axis=0), w_ada, b_ada)
    mod_p = mod[:, :, n_seq:].transpose(0, 2, 1, 3)
    sb = jnp.pad(state_conv_b, ((0, 0), (0, 0), (hb - (kb - 1), 0), (0, 0)))
    weights = _stacked_weights(g_mix, w_in, b_in, conv_a_w, conv_a_b, ln_a_g, ln_a_b, conv_b_w, conv_b_b,
                               w_gate_a, b_gate_a, w_gate_x, b_gate_x, lru_lambda, w_out, g_ffn, w_up, w_down,
                               g_final)
    xs, qa, qb, qh = _sample_call(x_sample.reshape(n_seq * steps, d), mod, state_conv_a, sb,
                                  state_lru_h.reshape(depth, n_seq, 1, wb), weights, dims)
    xp = x_prompt
    outs_p = []
    for l in range(depth):
        xp, pa, pb, ph = _prompt_call(xp, mod_p[l], weights, l, dims, l == depth - 1)
        outs_p.append((pa, pb, ph.reshape(batch, wb)))

    def stack(outs, i):
        return jnp.stack([o[i] for o in outs])

    return (xp, xs[depth - 1].reshape(n_seq, steps, d),
            stack(outs_p, 0), stack(outs_p, 1), stack(outs_p, 2),
            qa, qb, qh.reshape(depth, n_seq, wb))
```

```python
import functools

import jax
import jax.numpy as jnp
from jax import lax
from jax.experimental import pallas as pl
from jax.experimental.pallas import tpu as pltpu

F32 = jnp.float32
BF16 = jnp.bfloat16

SUBLANES = 8
LANES = 128
MXU_DIM = 256

RG_C = 8.0
EPS = 1e-6
GELU_C = 0.7978845608028654

PROMPT_ROWS = 256
SAMPLE_SEQS = 32
FF_CHUNK = 512
ADA_COLS = 1024
PACK_ELEMS = 1 << 20
VMEM_LIMIT = 52 * 1024 * 1024


def _sigmoid(x):
    return 1.0 / (1.0 + jnp.exp(-x))


def _silu(x):
    return x * _sigmoid(x)


def _gelu_tanh(x):
    return x * (0.5 * (1.0 + jnp.tanh(GELU_C * (x + 0.044715 * (x * x * x)))))


def _softplus(x):
    return jnp.maximum(x, 0.0) + jnp.log1p(jnp.exp(-jnp.abs(x)))


def _rms(x, g):
    return x * lax.rsqrt(jnp.mean(x * x, axis=-1, keepdims=True) + EPS) * g


def _layer_norm(x, g, b):
    mu = jnp.mean(x, axis=-1, keepdims=True)
    xc = x - mu
    var = jnp.mean(xc * xc, axis=-1, keepdims=True)
    return xc * lax.rsqrt(var + EPS) * g + b


def _dot(a, b):
    return jnp.dot(a, b, preferred_element_type=F32)


def _pack_body(w_ref, o_ref):
    o_ref[...] = pltpu.bitcast(w_ref[...].astype(BF16), jnp.uint32)


def _pack_rows(w):
    k, n = w.shape[-2:]
    flat = w.reshape(-1, n)
    rows = flat.shape[0]
    block = min(rows, PACK_ELEMS // n)
    assert k % 2 == 0 and rows % block == 0 and block % (2 * SUBLANES) == 0
    packed = pl.pallas_call(
        _pack_body,
        grid=(rows // block,),
        in_specs=[pl.BlockSpec((block, n), lambda i: (i, 0))],
        out_specs=pl.BlockSpec((block // 2, n), lambda i: (i, 0)),
        out_shape=jax.ShapeDtypeStruct((rows // 2, n), jnp.uint32),
        compiler_params=pltpu.CompilerParams(dimension_semantics=("arbitrary",)),
        name="pack_weights",
    )(flat)
    return packed.reshape(w.shape[:-2] + (k // 2, n))


def _wmat(ref, rows=slice(None), cols=slice(None), lead=()):
    start = None if rows.start is None else rows.start // 2
    stop = None if rows.stop is None else rows.stop // 2
    return pltpu.bitcast(ref[lead + (slice(start, stop), cols)], BF16)


def _conv_taps(width, hist_rows):
    lead = hist_rows - (width - 1)
    taps = {}
    for k in range(width):
        taps.setdefault((k + lead) % SUBLANES, []).append((k + lead) // SUBLANES)
    return lead, taps


def _conv_tile(tile_fn, w_ref, lanes, width, hist_rows, prev, row):
    lead, taps = _conv_taps(width, hist_rows)

    def z(r, off):
        acc = None
        for q in taps[r]:
            term = w_ref[SUBLANES * q + r - lead, :, lanes] * tile_fn(q + off)
            acc = term if acc is None else acc + term
        return acc

    shifted = [r for r in sorted(taps) if r != 0]
    new = [z(r, 1) for r in shifted]
    if prev is None:
        prev = [z(r, 0) for r in shifted]
    out = z(0, 0)
    for r, p, n in zip(shifted, prev, new):
        out = out + pltpu.roll(jnp.where(row >= r, p, n), SUBLANES - r, axis=0)
    return out, new


def _conv_steps(buf, w_ref, out_ref, n_rows, width, hist_rows, g):
    row = lax.broadcasted_iota(jnp.int32, (SUBLANES, LANES), 0)
    lanes = slice(g * LANES, (g + 1) * LANES)
    carry = {"prev": None}

    def make(j):
        def step():
            def tile_fn(i):
                return buf[(j + i) * SUBLANES:(j + i + 1) * SUBLANES, lanes]

            out, carry["prev"] = _conv_tile(tile_fn, w_ref, lanes, width, hist_rows, carry["prev"], row)
            out_ref[j * SUBLANES:(j + 1) * SUBLANES, lanes] = out

        return step

    return [make(j) for j in range(n_rows // SUBLANES)]


def _alternate(pieces, steps):
    total = sum(c for _, c in pieces)
    done = 0
    cost = 0
    for thunk, c in pieces:
        upto = len(steps) * (2 * cost + c) // (2 * total)
        for step in steps[done:upto]:
            step()
        done = upto
        thunk()
        cost += c
    for step in steps[done:]:
        step()


def _scan_tile(a, b, row):
    for s in (1, 2, 4):
        keep = row >= s
        a_prev = jnp.where(keep, pltpu.roll(a, s, axis=0), 1.0)
        b_prev = jnp.where(keep, pltpu.roll(b, s, axis=0), 0.0)
        b = a * b_prev + b
        a = a * a_prev
    return a, b


def _scan_rows(a_ref, b_ref, carry_ref, n_rows):
    channels = a_ref.shape[-1]
    row = lax.broadcasted_iota(jnp.int32, (SUBLANES, channels), 0)
    h_prev = carry_ref[...]
    for j in range(n_rows // SUBLANES):
        rows = slice(j * SUBLANES, (j + 1) * SUBLANES)
        a_cum, b_cum = _scan_tile(a_ref[rows, :], b_ref[rows, :], row)
        h = a_cum * h_prev + b_cum
        b_ref[rows, :] = h
        h_prev = jnp.broadcast_to(h[SUBLANES - 1:SUBLANES, :], (SUBLANES, channels))
    carry_ref[...] = h_prev


def _lru_inputs(cb, g, b_ga, b_gx, softplus_neg_lam, first_row):
    n = cb.shape[-1]
    gate_a = g[:, 0:n] + b_ga
    gate_x = g[:, n:2 * n] + b_gx
    log_a = -RG_C * _sigmoid(gate_a) * softplus_neg_lam
    a = jnp.exp(log_a)
    mult = jnp.sqrt(-jnp.tanh(log_a) * (a * a + 1.0))
    if first_row is not None:
        mult = jnp.where(first_row, 1.0, mult)
    return a, mult * _sigmoid(gate_x) * cb


def _mixer_head(w, x, shift, scale):
    return (_rms(x, w["g_mix"][...]) * (1.0 + scale) + shift).astype(BF16)


def _proj(w, h, c0, c1):
    return _dot(h, _wmat(w["w_in"], cols=slice(c0, c1))) + w["b_in"][:, c0:c1]


def _mixer_gates(w, cb_ref, a_ref, b_ref, first_row):
    cb = cb_ref[...] + w["conv_b_b"][...]
    cb_lo = cb.astype(BF16)
    sp = _softplus(-w["lam"][...])
    for kb in range(cb.shape[-1] // MXU_DIM):
        cs = slice(kb * MXU_DIM, (kb + 1) * MXU_DIM)
        g = _dot(cb_lo[:, cs], _wmat(w["w_gate"], lead=(kb,)))
        a, b = _lru_inputs(cb[:, cs], g, w["b_ga"][:, cs], w["b_gx"][:, cs], sp[:, cs], first_row)
        a_ref[:, cs] = a
        b_ref[:, cs] = b


def _mixer_tail(w, x, gate, ya, yb):
    wa = ya.shape[-1]
    wy = yb.shape[-1]
    mix = (_dot(ya.astype(BF16), _wmat(w["w_out"], rows=slice(0, wa)))
           + _dot(yb.astype(BF16), _wmat(w["w_out"], rows=slice(wa, wa + wy))))
    return x + gate * mix


def _ffn_head(w, x, shift, scale):
    return (_rms(x, w["g_ffn"][...]) * (1.0 + scale) + shift).astype(BF16)


def _ffn_chunks(w, h, acc_ref, first, last):
    for j in range(first, last):
        cols = slice(j * FF_CHUNK, (j + 1) * FF_CHUNK)
        act = jnp.square(jnp.maximum(_dot(h, _wmat(w["w_up"], cols=cols)), 0.0)).astype(BF16)
        part = _dot(act, _wmat(w["w_down"], rows=cols))
        if j == 0:
            acc_ref[...] = part
        else:
            acc_ref[...] += part


def _ffn_pieces(w, hf_ref, act_ref, acc_ref, j):
    cols = slice(j * FF_CHUNK, (j + 1) * FF_CHUNK)
    half = acc_ref.shape[-1] // 2

    def up():
        act_ref[...] = jnp.square(jnp.maximum(_dot(hf_ref[...], _wmat(w["w_up"], cols=cols)), 0.0)).astype(BF16)

    def down(n):
        def piece():
            c = slice(n * half, (n + 1) * half)
            part = _dot(act_ref[...], _wmat(w["w_down"], rows=cols, cols=c))
            if j == 0:
                acc_ref[:, c] = part
            else:
                acc_ref[:, c] += part

        return piece

    return [(up, 2), (down(0), 1), (down(1), 1)]


def _ffn_tail(w, x, gate, acc_ref, final_norm):
    out = x + gate * acc_ref[...]
    if final_norm:
        out = _rms(out, w["g_final"][...])
    return out


_WEIGHT_NAMES = ("g_mix", "w_in", "b_in", "conv_a_w", "conv_a_b", "ln_g", "ln_b", "conv_b_w", "conv_b_b",
                 "w_gate", "b_ga", "b_gx", "lam", "w_out", "g_ffn", "w_up", "w_down", "g_final")


def _prompt_body(final_norm, dims, chunks_per_seq, n_chunks, go_ref, x_ref, mod_ref, mod_ffn_ref, *rest):
    ka, kb, ha, hb = dims
    nw = len(_WEIGHT_NAMES)
    w = dict(zip(_WEIGHT_NAMES, rest[:nw]))
    (xo_ref, na_ref, nb_ref, nh_ref, gbuf, rbuf, cabuf, cbbuf, abuf, bbuf, hcar,
     hbuf, hfbuf, yabuf, gybuf, x1_new, x1_old, ffacc, act0, act1) = rest[nw:]
    acts = (act0, act1)
    n_rows = x_ref.shape[0]
    wa = gbuf.shape[-1]
    wb = rbuf.shape[-1]
    s = pl.program_id(0)
    t = jnp.minimum(s, n_chunks - 1) % chunks_per_seq
    n_ff = w["w_up"].shape[-1] // FF_CHUNK
    assert n_ff == 8

    @pl.when(s == 0)
    def _():
        x1_old[...] = jnp.zeros_like(x1_old)

    @pl.when(t == 0)
    def _():
        gbuf[0:ha, :] = jnp.zeros((ha, wa), F32)
        rbuf[0:hb, :] = jnp.zeros((hb, wb), F32)
        hcar[...] = jnp.zeros_like(hcar)

    h = _mixer_head(w, x_ref[...], mod_ref[0:1, :], mod_ref[1:2, :])
    hbuf[...] = h
    gbuf[ha:ha + n_rows, :] = _proj(w, h, 0, wa) * _sigmoid(_proj(w, h, wa, 2 * wa))
    rbuf[hb:hb + n_rows, :] = _proj(w, h, 2 * wa, 2 * wa + wb)
    hfbuf[...] = _ffn_head(w, x1_old[...], mod_ffn_ref[3:4, :], mod_ffn_ref[4:5, :])

    @pl.when(go_ref[0] != 0)
    def _():
        steps = []
        for g in range(wa // LANES):
            conv_a = _conv_steps(gbuf, w["conv_a_w"], cabuf, n_rows, ka, ha, g)
            conv_b = _conv_steps(rbuf, w["conv_b_w"], cbbuf, n_rows, kb, hb, g)
            steps += [f for pair in zip(conv_a, conv_b) for f in pair]
        f0, f1, f2, f3 = (_ffn_pieces(w, hfbuf, acts[j % 2], ffacc, j) for j in range(4))
        order = [f0[0], f1[0], f0[1], f0[2], f2[0], f1[1], f1[2], f3[0], f2[1], f2[2], f3[1], f3[2]]
        _alternate(order, steps)

    ya = _silu(_layer_norm(cabuf[...] + w["conv_a_b"][...], w["ln_g"][...], w["ln_b"][...]))
    yabuf[...] = ya.astype(BF16)
    first_row = (lax.broadcasted_iota(jnp.int32, (n_rows, 1), 0) + t * n_rows) == 0
    _mixer_gates(w, cbbuf, abuf, bbuf, first_row)
    gybuf[...] = _gelu_tanh(_proj(w, hbuf[...], 2 * wa + wb, 2 * wa + 2 * wb))
    _ffn_chunks(w, hfbuf[...], ffacc, 4, 5)
    _scan_rows(abuf, bbuf, hcar, n_rows)
    _ffn_chunks(w, hfbuf[...], ffacc, 5, 6)
    yb = bbuf[...] * gybuf[...]
    x1_new[...] = _mixer_tail(w, x_ref[...], mod_ref[2:3, :], yabuf[...], yb)
    _ffn_chunks(w, hfbuf[...], ffacc, 6, n_ff)
    xo_ref[...] = _ffn_tail(w, x1_old[...], mod_ffn_ref[5:6, :], ffacc, final_norm)
    x1_old[...] = x1_new[...]
    gbuf[0:ha, :] = gbuf[n_rows:n_rows + ha, :]
    rbuf[0:hb, :] = rbuf[n_rows:n_rows + hb, :]

    @pl.when((t == chunks_per_seq - 1) & (s < n_chunks))
    def _():
        na_ref[...] = gbuf[ha - (ka - 1):ha, :]
        nb_ref[...] = rbuf[hb - (kb - 1):hb, :]
        nh_ref[...] = hcar[0:1, :]


def _sample_body(dims, x_ref, mod_ref, sa_ref, sb_ref, sh_ref, *rest):
    ka, kb, ha, hb = dims
    nw = len(_WEIGHT_NAMES)
    w = dict(zip(_WEIGHT_NAMES, rest[:nw]))
    xo_ref, na_ref, nb_ref, nh_ref, xres, gabuf, urbuf, cabuf, cbbuf, abuf, bbuf, ffacc = rest[nw:]
    layer = pl.program_id(0)
    blk = pl.program_id(1)
    n_rows, d = x_ref.shape
    n_seq = sa_ref.shape[0]
    steps = n_rows // n_seq
    wa = gabuf.shape[-1]
    wb = urbuf.shape[-1]
    lead_a = ha - (ka - 1)

    @pl.when(layer == 0)
    def _():
        xres[blk] = x_ref[...]

    def per_token(i):
        return jnp.broadcast_to(mod_ref[i][:, None, :], (n_seq, steps, d)).reshape(n_rows, d)

    x = xres[blk]
    h = _mixer_head(w, x, per_token(0), per_token(1))
    gabuf[...] = _proj(w, h, 0, wa) * _sigmoid(_proj(w, h, wa, 2 * wa))
    urbuf[...] = _proj(w, h, 2 * wa, 2 * wa + wb)
    row = lax.broadcasted_iota(jnp.int32, (SUBLANES, LANES), 0)

    def conv_seq(b, carry):
        rows = pl.ds(pl.multiple_of(b * SUBLANES, SUBLANES), SUBLANES)
        for g in range(wa // LANES):
            lanes = slice(g * LANES, (g + 1) * LANES)

            def tile_a(i):
                if i == 0:
                    return pltpu.roll(sa_ref[b, 0:SUBLANES, lanes], lead_a, axis=0)
                if i < ha // SUBLANES:
                    return sa_ref[b, i * SUBLANES - lead_a:(i + 1) * SUBLANES - lead_a, lanes]
                return gabuf[rows, lanes]

            cabuf[rows, lanes], _ = _conv_tile(tile_a, w["conv_a_w"], lanes, ka, ha, None, row)
        for g in range(wb // LANES):
            lanes = slice(g * LANES, (g + 1) * LANES)

            def tile_b(i):
                if i < hb // SUBLANES:
                    return sb_ref[b, i * SUBLANES:(i + 1) * SUBLANES, lanes]
                return urbuf[rows, lanes]

            cbbuf[rows, lanes], _ = _conv_tile(tile_b, w["conv_b_w"], lanes, kb, hb, None, row)
        na_ref[b, 0:ka - 1 - steps, :] = sa_ref[b, steps:ka - 1, :]
        na_ref[b, ka - 1 - steps:ka - 1, :] = gabuf[rows, :]
        nb_ref[b] = urbuf[rows, :][steps - (kb - 1):steps, :]
        return carry

    lax.fori_loop(0, n_seq, conv_seq, 0)
    ya = _silu(_layer_norm(cabuf[...] + w["conv_a_b"][...], w["ln_g"][...], w["ln_b"][...]))
    _mixer_gates(w, cbbuf, abuf, bbuf, None)
    row_b = lax.broadcasted_iota(jnp.int32, (SUBLANES, wb), 0)

    def scan_seq(b, carry):
        rows = pl.ds(pl.multiple_of(b * SUBLANES, SUBLANES), SUBLANES)
        a_cum, b_cum = _scan_tile(abuf[rows, :], bbuf[rows, :], row_b)
        hs = a_cum * sh_ref[b] + b_cum
        bbuf[rows, :] = hs
        nh_ref[b] = hs[SUBLANES - 1:SUBLANES, :]
        return carry

    lax.fori_loop(0, n_seq, scan_seq, 0, unroll=2)
    yb = bbuf[...] * _gelu_tanh(_proj(w, h, 2 * wa + wb, 2 * wa + 2 * wb))
    x1 = _mixer_tail(w, x, per_token(2), ya, yb)
    h_ffn = _ffn_head(w, x1, per_token(3), per_token(4))
    _ffn_chunks(w, h_ffn, ffacc, 0, w["w_up"].shape[-1] // FF_CHUNK)
    out = _ffn_tail(w, x1, per_token(5), ffacc, False)
    xres[blk] = out
    xo_ref[...] = jnp.where(layer == pl.num_programs(0) - 1, _rms(out, w["g_final"][...]), out)


def _ada_body(c_ref, w_ref, b_ref, o_ref):
    c = c_ref[...]
    o_ref[...] = _dot(_silu(c).astype(BF16), w_ref[...].astype(BF16)) + b_ref[...]


def _ada_call(c_all, w_ada, b_ada):
    depth, d, n = w_ada.shape
    rows = c_all.shape[0]
    assert ADA_COLS == d
    return pl.pallas_call(
        _ada_body,
        grid=(depth, n // d),
        in_specs=[
            pl.BlockSpec((rows, d), lambda l, j: (0, 0)),
            pl.BlockSpec((None, d, d), lambda l, j: (l, 0, j)),
            pl.BlockSpec((None, 1, d), lambda l, j: (l, 0, j)),
        ],
        out_specs=pl.BlockSpec((None, None, rows, d), lambda l, j: (l, j, 0, 0)),
        out_shape=jax.ShapeDtypeStruct((depth, n // d, rows, d), F32),
        compiler_params=pltpu.CompilerParams(dimension_semantics=("arbitrary", "arbitrary")),
        name="ada_mod",
    )(c_all, w_ada, b_ada.reshape(depth, 1, n))


def _resident(arr, layer=None):
    tail = (0,) * (arr.ndim - 1)
    index_map = (lambda l, i: (l,) + tail) if layer is None else (lambda s: (layer,) + tail)
    return pl.BlockSpec((None,) + arr.shape[1:], index_map, pipeline_mode=pl.Buffered(1))


def _prompt_call(x, mod, weights, layer, dims, final_norm):
    batch, seq, d = x.shape
    ka, kb, ha, hb = dims
    wa = weights["conv_a_b"].shape[-1]
    wb = weights["conv_b_b"].shape[-1]
    r = PROMPT_ROWS
    per_seq = seq // r
    n_chunks = batch * per_seq
    ws = [weights[k] for k in _WEIGHT_NAMES]

    def mixer_chunk(s):
        return jnp.minimum(s, n_chunks - 1)

    def ffn_chunk(s):
        return jnp.maximum(s - 1, 0)

    return pl.pallas_call(
        functools.partial(_prompt_body, final_norm, dims, per_seq, n_chunks),
        grid=(n_chunks + 1,),
        in_specs=[
            pl.BlockSpec(memory_space=pltpu.SMEM),
            pl.BlockSpec((None, r, d), lambda s: (mixer_chunk(s) // per_seq, mixer_chunk(s) % per_seq, 0)),
            pl.BlockSpec((None, 6, d), lambda s: (mixer_chunk(s) // per_seq, 0, 0)),
            pl.BlockSpec((None, 6, d), lambda s: (ffn_chunk(s) // per_seq, 0, 0)),
        ] + [_resident(a, layer) for a in ws],
        out_specs=[
            pl.BlockSpec((None, r, d), lambda s: (ffn_chunk(s) // per_seq, ffn_chunk(s) % per_seq, 0)),
            pl.BlockSpec((None, ka - 1, wa), lambda s: (mixer_chunk(s) // per_seq, 0, 0)),
            pl.BlockSpec((None, kb - 1, wb), lambda s: (mixer_chunk(s) // per_seq, 0, 0)),
            pl.BlockSpec((None, 1, wb), lambda s: (mixer_chunk(s) // per_seq, 0, 0)),
        ],
        out_shape=[
            jax.ShapeDtypeStruct((batch, seq, d), F32),
            jax.ShapeDtypeStruct((batch, ka - 1, wa), F32),
            jax.ShapeDtypeStruct((batch, kb - 1, wb), F32),
            jax.ShapeDtypeStruct((batch, 1, wb), F32),
        ],
        scratch_shapes=[
            pltpu.VMEM((ha + r, wa), F32),
            pltpu.VMEM((hb + r, wb), F32),
            pltpu.VMEM((r, wa), F32),
            pltpu.VMEM((r, wb), F32),
            pltpu.VMEM((r, wb), F32),
            pltpu.VMEM((r, wb), F32),
            pltpu.VMEM((SUBLANES, wb), F32),
            pltpu.VMEM((r, d), BF16),
            pltpu.VMEM((r, d), BF16),
            pltpu.VMEM((r, wa), BF16),
            pltpu.VMEM((r, wb), F32),
            pltpu.VMEM((r, d), F32),
            pltpu.VMEM((r, d), F32),
            pltpu.VMEM((r, d), F32),
            pltpu.VMEM((r, FF_CHUNK), BF16),
            pltpu.VMEM((r, FF_CHUNK), BF16),
        ],
        compiler_params=pltpu.CompilerParams(
            dimension_semantics=("arbitrary",), vmem_limit_bytes=VMEM_LIMIT),
        name="prompt_layer",
    )(jnp.ones((1,), jnp.int32), x, mod, mod, *ws)


def _sample_call(x, mod, state_a, state_b, state_h, weights, dims):
    n_rows, d = x.shape
    depth, n_seq = state_a.shape[:2]
    steps = n_rows // n_seq
    ka, kb, ha, hb = dims
    wa = weights["conv_a_b"].shape[-1]
    wb = weights["conv_b_b"].shape[-1]
    bt = SAMPLE_SEQS
    r = bt * steps
    ws = [weights[k] for k in _WEIGHT_NAMES]

    def per_block(shape):
        return pl.BlockSpec((None, bt) + shape, lambda l, i: (l, i, 0, 0))

    return pl.pallas_call(
        functools.partial(_sample_body, dims),
        grid=(depth, n_seq // bt),
        in_specs=[
            pl.BlockSpec((r, d), lambda l, i: (i, 0)),
            pl.BlockSpec((None, 6, bt, d), lambda l, i: (l, 0, i, 0)),
            per_block((ka - 1, wa)),
            per_block((hb, wb)),
            per_block((1, wb)),
        ] + [_resident(a) for a in ws],
        out_specs=[
            pl.BlockSpec((None, r, d), lambda l, i: (l, i, 0)),
            per_block((ka - 1, wa)),
            per_block((kb - 1, wb)),
            per_block((1, wb)),
        ],
        out_shape=[
            jax.ShapeDtypeStruct((depth, n_rows, d), F32),
            jax.ShapeDtypeStruct((depth, n_seq, ka - 1, wa), F32),
            jax.ShapeDtypeStruct((depth, n_seq, kb - 1, wb), F32),
            jax.ShapeDtypeStruct((depth, n_seq, 1, wb), F32),
        ],
        scratch_shapes=[pltpu.VMEM((n_seq // bt, r, d), F32)]
        + [pltpu.VMEM((r, wa), F32), pltpu.VMEM((r, wb), F32)] * 3 + [pltpu.VMEM((r, d), F32)],
        compiler_params=pltpu.CompilerParams(
            dimension_semantics=("arbitrary", "arbitrary"), vmem_limit_bytes=VMEM_LIMIT),
        name="sample_layers",
    )(x, mod, state_a, state_b, state_h, *ws)


def _hist_rows(width):
    return -(-(width - 1) // SUBLANES) * SUBLANES


def _gate_blocks(w_ga, w_gx):
    depth, heads, hd, _ = w_ga.shape
    per_block = MXU_DIM // hd
    blocks = jnp.zeros((depth, heads // per_block, MXU_DIM, 2 * MXU_DIM), F32)
    for h in range(heads):
        k, i = divmod(h, per_block)
        rows = slice(i * hd, (i + 1) * hd)
        blocks = blocks.at[:, k, rows, i * hd:(i + 1) * hd].set(w_ga[:, h])
        blocks = blocks.at[:, k, rows, MXU_DIM + i * hd:MXU_DIM + (i + 1) * hd].set(w_gx[:, h])
    return blocks


def _stacked_weights(g_mix, w_in, b_in, conv_a_w, conv_a_b, ln_a_g, ln_a_b, conv_b_w, conv_b_b,
                     w_gate_a, b_gate_a, w_gate_x, b_gate_x, lru_lambda, w_out, g_ffn, w_up, w_down, g_final):
    depth = w_in.shape[0]

    def row(v):
        return v.reshape(depth, 1, -1)

    def taps(wk):
        return jnp.broadcast_to(wk[:, :, None, :], wk.shape[:2] + (SUBLANES, wk.shape[2]))

    return {
        "g_mix": row(g_mix), "w_in": _pack_rows(w_in), "b_in": row(b_in),
        "conv_a_w": taps(conv_a_w), "conv_a_b": row(conv_a_b), "ln_g": row(ln_a_g), "ln_b": row(ln_a_b),
        "conv_b_w": taps(conv_b_w), "conv_b_b": row(conv_b_b),
        "w_gate": _pack_rows(_gate_blocks(w_gate_a, w_gate_x)),
        "b_ga": row(b_gate_a), "b_gx": row(b_gate_x), "lam": row(lru_lambda),
        "w_out": _pack_rows(w_out), "g_ffn": row(g_ffn),
        "w_up": _pack_rows(w_up), "w_down": _pack_rows(w_down),
        "g_final": jnp.broadcast_to(g_final.reshape(1, 1, -1), (depth, 1, g_final.shape[-1])),
    }


def kernel(x_prompt, x_sample, c_prompt, c_sample, state_conv_a, state_conv_b, state_lru_h, w_ada, b_ada, g_mix, w_in, b_in, conv_a_w, conv_a_b, ln_a_g, ln_a_b, conv_b_w, conv_b_b, w_gate_a, b_gate_a, w_gate_x, b_gate_x, lru_lambda, w_out, g_ffn, w_up, w_down, g_final):
    batch, seq, d = x_prompt.shape
    n_seq, steps, _ = x_sample.shape
    depth = w_ada.shape[0]
    ka, wa = conv_a_w.shape[1:]
    kb, wb = conv_b_w.shape[1:]
    ha, hb = _hist_rows(ka), _hist_rows(kb)
    dims = (ka, kb, ha, hb)
    assert steps == SUBLANES and seq % PROMPT_ROWS == 0 and n_seq % SAMPLE_SEQS == 0
    assert wa % MXU_DIM == 0 and wb % MXU_DIM == 0 and MXU_DIM % w_gate_a.shape[-1] == 0

    mod = _ada_call(jnp.concatenate([c_sample, c_prompt], axis=0), w_ada, b_ada)
    mod_p = mod[:, :, n_seq:].transpose(0, 2, 1, 3)
    sb = jnp.pad(state_conv_b, ((0, 0), (0, 0), (hb - (kb - 1), 0), (0, 0)))
    weights = _stacked_weights(g_mix, w_in, b_in, conv_a_w, conv_a_b, ln_a_g, ln_a_b, conv_b_w, conv_b_b,
                               w_gate_a, b_gate_a, w_gate_x, b_gate_x, lru_lambda, w_out, g_ffn, w_up, w_down,
                               g_final)
    xs, qa, qb, qh = _sample_call(x_sample.reshape(n_seq * steps, d), mod, state_conv_a, sb,
                                  state_lru_h.reshape(depth, n_seq, 1, wb), weights, dims)
    xp = x_prompt
    outs_p = []
    for l in range(depth):
        xp, pa, pb, ph = _prompt_call(xp, mod_p[l], weights, l, dims, l == depth - 1)
        outs_p.append((pa, pb, ph.reshape(batch, wb)))

    def stack(outs, i):
        return jnp.stack([o[i] for o in outs])

    return (xp, xs[depth - 1].reshape(n_seq, steps, d),
            stack(outs_p, 0), stack(outs_p, 1), stack(outs_p, 2),
            qa, qb, qh.reshape(depth, n_seq, wb))
```

```python
import functools

import jax
import jax.numpy as jnp
from jax import lax
from jax.experimental import pallas as pl
from jax.experimental.pallas import tpu as pltpu

F32 = jnp.float32
BF16 = jnp.bfloat16

SUBLANES = 8
LANES = 128
MXU_DIM = 256

RG_C = 8.0
EPS = 1e-6
GELU_C = 0.7978845608028654

PROMPT_ROWS = 256
SAMPLE_SEQS = 32
FF_CHUNK = 512
ADA_COLS = 1024
PACK_ELEMS = 1 << 20
VMEM_LIMIT = 52 * 1024 * 1024


def _sigmoid(x):
    return 1.0 / (1.0 + jnp.exp(-x))


def _silu(x):
    return x * _sigmoid(x)


def _gelu_tanh(x):
    return x * (0.5 * (1.0 + jnp.tanh(GELU_C * (x + 0.044715 * (x * x * x)))))


def _softplus(x):
    return jnp.maximum(x, 0.0) + jnp.log1p(jnp.exp(-jnp.abs(x)))


def _rms(x, g):
    return x * lax.rsqrt(jnp.mean(x * x, axis=-1, keepdims=True) + EPS) * g


def _layer_norm(x, g, b):
    mu = jnp.mean(x, axis=-1, keepdims=True)
    xc = x - mu
    var = jnp.mean(xc * xc, axis=-1, keepdims=True)
    return xc * lax.rsqrt(var + EPS) * g + b


def _dot(a, b):
    return jnp.dot(a, b, preferred_element_type=F32)


def _pack_body(w_ref, o_ref):
    o_ref[...] = pltpu.bitcast(w_ref[...].astype(BF16), jnp.uint32)


def _pack_rows(w):
    k, n = w.shape[-2:]
    flat = w.reshape(-1, n)
    rows = flat.shape[0]
    block = min(rows, PACK_ELEMS // n)
    assert k % 2 == 0 and rows % block == 0 and block % (2 * SUBLANES) == 0
    packed = pl.pallas_call(
        _pack_body,
        grid=(rows // block,),
        in_specs=[pl.BlockSpec((block, n), lambda i: (i, 0))],
        out_specs=pl.BlockSpec((block // 2, n), lambda i: (i, 0)),
        out_shape=jax.ShapeDtypeStruct((rows // 2, n), jnp.uint32),
        compiler_params=pltpu.CompilerParams(dimension_semantics=("arbitrary",)),
        name="pack_weights",
    )(flat)
    return packed.reshape(w.shape[:-2] + (k // 2, n))


def _wmat(ref, rows=slice(None), cols=slice(None), lead=()):
    start = None if rows.start is None else rows.start // 2
    stop = None if rows.stop is None else rows.stop // 2
    return pltpu.bitcast(ref[lead + (slice(start, stop), cols)], BF16)


def _conv_taps(width, hist_rows):
    lead = hist_rows - (width - 1)
    taps = {}
    for k in range(width):
        taps.setdefault((k + lead) % SUBLANES, []).append((k + lead) // SUBLANES)
    return lead, taps


def _conv_tile(tile_fn, w_ref, lanes, width, hist_rows, prev, row):
    lead, taps = _conv_taps(width, hist_rows)

    def z(r, off):
        acc = None
        for q in taps[r]:
            term = w_ref[SUBLANES * q + r - lead, :, lanes] * tile_fn(q + off)
            acc = term if acc is None else acc + term
        return acc

    shifted = [r for r in sorted(taps) if r != 0]
    new = [z(r, 1) for r in shifted]
    if prev is None:
        prev = [z(r, 0) for r in shifted]
    out = z(0, 0)
    for r, p, n in zip(shifted, prev, new):
        out = out + pltpu.roll(jnp.where(row >= r, p, n), SUBLANES - r, axis=0)
    return out, new


def _conv_steps(buf, w_ref, out_ref, n_rows, width, hist_rows, g):
    row = lax.broadcasted_iota(jnp.int32, (SUBLANES, LANES), 0)
    lanes = slice(g * LANES, (g + 1) * LANES)
    carry = {"prev": None}

    def make(j):
        def step():
            def tile_fn(i):
                return buf[(j + i) * SUBLANES:(j + i + 1) * SUBLANES, lanes]

            out, carry["prev"] = _conv_tile(tile_fn, w_ref, lanes, width, hist_rows, carry["prev"], row)
            out_ref[j * SUBLANES:(j + 1) * SUBLANES, lanes] = out

        return step

    return [make(j) for j in range(n_rows // SUBLANES)]


def _alternate(pieces, steps):
    total = sum(c for _, c in pieces)
    done = 0
    cost = 0
    for thunk, c in pieces:
        upto = len(steps) * (2 * cost + c) // (2 * total)
        for step in steps[done:upto]:
            step()
        done = upto
        thunk()
        cost += c
    for step in steps[done:]:
        step()


def _scan_tile(a, b, row):
    for s in (1, 2, 4):
        keep = row >= s
        a_prev = jnp.where(keep, pltpu.roll(a, s, axis=0), 1.0)
        b_prev = jnp.where(keep, pltpu.roll(b, s, axis=0), 0.0)
        b = a * b_prev + b
        a = a * a_prev
    return a, b


def _scan_rows(a_ref, b_ref, carry_ref, n_rows):
    channels = a_ref.shape[-1]
    row = lax.broadcasted_iota(jnp.int32, (SUBLANES, channels), 0)
    h_prev = carry_ref[...]
    for j in range(n_rows // SUBLANES):
        rows = slice(j * SUBLANES, (j + 1) * SUBLANES)
        a_cum, b_cum = _scan_tile(a_ref[rows, :], b_ref[rows, :], row)
        h = a_cum * h_prev + b_cum
        b_ref[rows, :] = h
        h_prev = jnp.broadcast_to(h[SUBLANES - 1:SUBLANES, :], (SUBLANES, channels))
    carry_ref[...] = h_prev


def _lru_inputs(cb, g, b_ga, b_gx, softplus_neg_lam, first_row):
    n = cb.shape[-1]
    gate_a = g[:, 0:n] + b_ga
    gate_x = g[:, n:2 * n] + b_gx
    log_a = -RG_C * _sigmoid(gate_a) * softplus_neg_lam
    a = jnp.exp(log_a)
    mult = jnp.sqrt(-jnp.tanh(log_a) * (a * a + 1.0))
    if first_row is not None:
        mult = jnp.where(first_row, 1.0, mult)
    return a, mult * _sigmoid(gate_x) * cb


def _mixer_head(w, x, shift, scale):
    return (_rms(x, w["g_mix"][...]) * (1.0 + scale) + shift).astype(BF16)


def _proj(w, h, c0, c1):
    return _dot(h, _wmat(w["w_in"], cols=slice(c0, c1))) + w["b_in"][:, c0:c1]


def _mixer_gates(w, cb_ref, a_ref, b_ref, first_row):
    cb = cb_ref[...] + w["conv_b_b"][...]
    cb_lo = cb.astype(BF16)
    sp = _softplus(-w["lam"][...])
    for kb in range(cb.shape[-1] // MXU_DIM):
        cs = slice(kb * MXU_DIM, (kb + 1) * MXU_DIM)
        g = _dot(cb_lo[:, cs], _wmat(w["w_gate"], lead=(kb,)))
        a, b = _lru_inputs(cb[:, cs], g, w["b_ga"][:, cs], w["b_gx"][:, cs], sp[:, cs], first_row)
        a_ref[:, cs] = a
        b_ref[:, cs] = b


def _mixer_tail(w, x, gate, ya, yb):
    wa = ya.shape[-1]
    wy = yb.shape[-1]
    mix = (_dot(ya.astype(BF16), _wmat(w["w_out"], rows=slice(0, wa)))
           + _dot(yb.astype(BF16), _wmat(w["w_out"], rows=slice(wa, wa + wy))))
    return x + gate * mix


def _ffn_head(w, x, shift, scale):
    return (_rms(x, w["g_ffn"][...]) * (1.0 + scale) + shift).astype(BF16)


def _ffn_chunks(w, h, acc_ref, first, last):
    for j in range(first, last):
        cols = slice(j * FF_CHUNK, (j + 1) * FF_CHUNK)
        act = jnp.square(jnp.maximum(_dot(h, _wmat(w["w_up"], cols=cols)), 0.0)).astype(BF16)
        part = _dot(act, _wmat(w["w_down"], rows=cols))
        if j == 0:
            acc_ref[...] = part
        else:
            acc_ref[...] += part


def _ffn_pieces(w, hf_ref, act_ref, acc_ref, j):
    cols = slice(j * FF_CHUNK, (j + 1) * FF_CHUNK)
    half = acc_ref.shape[-1] // 2

    def up():
        act_ref[...] = jnp.square(jnp.maximum(_dot(hf_ref[...], _wmat(w["w_up"], cols=cols)), 0.0)).astype(BF16)

    def down(n):
        def piece():
            c = slice(n * half, (n + 1) * half)
            part = _dot(act_ref[...], _wmat(w["w_down"], rows=cols, cols=c))
            if j == 0:
                acc_ref[:, c] = part
            else:
                acc_ref[:, c] += part

        return piece

    return [(up, 2), (down(0), 1), (down(1), 1)]


def _ffn_tail(w, x, gate, acc_ref, final_norm):
    out = x + gate * acc_ref[...]
    if final_norm:
        out = _rms(out, w["g_final"][...])
    return out


_WEIGHT_NAMES = ("g_mix", "w_in", "b_in", "conv_a_w", "conv_a_b", "ln_g", "ln_b", "conv_b_w", "conv_b_b",
                 "w_gate", "b_ga", "b_gx", "lam", "w_out", "g_ffn", "w_up", "w_down", "g_final")


def _prompt_body(final_norm, dims, chunks_per_seq, n_chunks, go_ref, x_ref, mod_ref, mod_ffn_ref, *rest):
    ka, kb, ha, hb = dims
    nw = len(_WEIGHT_NAMES)
    w = dict(zip(_WEIGHT_NAMES, rest[:nw]))
    (xo_ref, na_ref, nb_ref, nh_ref, gbuf, rbuf, cabuf, cbbuf, abuf, bbuf, hcar,
     hbuf, hfbuf, yabuf, gybuf, x1_new, x1_old, ffacc, act0, act1) = rest[nw:]
    acts = (act0, act1)
    n_rows = x_ref.shape[0]
    wa = gbuf.shape[-1]
    wb = rbuf.shape[-1]
    s = pl.program_id(0)
    t = jnp.minimum(s, n_chunks - 1) % chunks_per_seq
    n_ff = w["w_up"].shape[-1] // FF_CHUNK
    assert n_ff == 8

    @pl.when(s == 0)
    def _():
        x1_old[...] = jnp.zeros_like(x1_old)

    @pl.when(t == 0)
    def _():
        gbuf[0:ha, :] = jnp.zeros((ha, wa), F32)
        rbuf[0:hb, :] = jnp.zeros((hb, wb), F32)
        hcar[...] = jnp.zeros_like(hcar)

    h = _mixer_head(w, x_ref[...], mod_ref[0:1, :], mod_ref[1:2, :])
    hbuf[...] = h
    gbuf[ha:ha + n_rows, :] = _proj(w, h, 0, wa) * _sigmoid(_proj(w, h, wa, 2 * wa))
    rbuf[hb:hb + n_rows, :] = _proj(w, h, 2 * wa, 2 * wa + wb)
    hfbuf[...] = _ffn_head(w, x1_old[...], mod_ffn_ref[3:4, :], mod_ffn_ref[4:5, :])

    @pl.when(go_ref[0] != 0)
    def _():
        steps = []
        for g in range(wa // LANES):
            conv_a = _conv_steps(gbuf, w["conv_a_w"], cabuf, n_rows, ka, ha, g)
            conv_b = _conv_steps(rbuf, w["conv_b_w"], cbbuf, n_rows, kb, hb, g)
            steps += [f for pair in zip(conv_a, conv_b) for f in pair]
        f0, f1, f2, f3, f4 = (_ffn_pieces(w, hfbuf, acts[j % 2], ffacc, j) for j in range(5))
        order = [f0[0], f1[0], f0[1], f0[2], f2[0], f1[1], f1[2], f3[0], f2[1], f2[2],
                 f4[0], f3[1], f3[2], f4[1], f4[2]]
        _alternate(order, steps)

    ya = _silu(_layer_norm(cabuf[...] + w["conv_a_b"][...], w["ln_g"][...], w["ln_b"][...]))
    yabuf[...] = ya.astype(BF16)
    first_row = (lax.broadcasted_iota(jnp.int32, (n_rows, 1), 0) + t * n_rows) == 0
    _mixer_gates(w, cbbuf, abuf, bbuf, first_row)
    gybuf[...] = _gelu_tanh(_proj(w, hbuf[...], 2 * wa + wb, 2 * wa + 2 * wb))
    _ffn_chunks(w, hfbuf[...], ffacc, 5, 6)
    _scan_rows(abuf, bbuf, hcar, n_rows)
    yb = bbuf[...] * gybuf[...]
    x1_new[...] = _mixer_tail(w, x_ref[...], mod_ref[2:3, :], yabuf[...], yb)
    _ffn_chunks(w, hfbuf[...], ffacc, 6, n_ff)
    xo_ref[...] = _ffn_tail(w, x1_old[...], mod_ffn_ref[5:6, :], ffacc, final_norm)
    x1_old[...] = x1_new[...]
    gbuf[0:ha, :] = gbuf[n_rows:n_rows + ha, :]
    rbuf[0:hb, :] = rbuf[n_rows:n_rows + hb, :]

    @pl.when((t == chunks_per_seq - 1) & (s < n_chunks))
    def _():
        na_ref[...] = gbuf[ha - (ka - 1):ha, :]
        nb_ref[...] = rbuf[hb - (kb - 1):hb, :]
        nh_ref[...] = hcar[0:1, :]


def _sample_body(dims, x_ref, mod_ref, sa_ref, sb_ref, sh_ref, *rest):
    ka, kb, ha, hb = dims
    nw = len(_WEIGHT_NAMES)
    w = dict(zip(_WEIGHT_NAMES, rest[:nw]))
    xo_ref, na_ref, nb_ref, nh_ref, xres, gabuf, urbuf, cabuf, cbbuf, abuf, bbuf, ffacc = rest[nw:]
    layer = pl.program_id(0)
    blk = pl.program_id(1)
    n_rows, d = x_ref.shape
    n_seq = sa_ref.shape[0]
    steps = n_rows // n_seq
    wa = gabuf.shape[-1]
    wb = urbuf.shape[-1]
    lead_a = ha - (ka - 1)

    @pl.when(layer == 0)
    def _():
        xres[blk] = x_ref[...]

    def per_token(i):
        return jnp.broadcast_to(mod_ref[i][:, None, :], (n_seq, steps, d)).reshape(n_rows, d)

    x = xres[blk]
    h = _mixer_head(w, x, per_token(0), per_token(1))
    gabuf[...] = _proj(w, h, 0, wa) * _sigmoid(_proj(w, h, wa, 2 * wa))
    urbuf[...] = _proj(w, h, 2 * wa, 2 * wa + wb)
    row = lax.broadcasted_iota(jnp.int32, (SUBLANES, LANES), 0)

    def conv_seq(b, carry):
        rows = pl.ds(pl.multiple_of(b * SUBLANES, SUBLANES), SUBLANES)
        for g in range(wa // LANES):
            lanes = slice(g * LANES, (g + 1) * LANES)

            def tile_a(i):
                if i == 0:
                    return pltpu.roll(sa_ref[b, 0:SUBLANES, lanes], lead_a, axis=0)
                if i < ha // SUBLANES:
                    return sa_ref[b, i * SUBLANES - lead_a:(i + 1) * SUBLANES - lead_a, lanes]
                return gabuf[rows, lanes]

            cabuf[rows, lanes], _ = _conv_tile(tile_a, w["conv_a_w"], lanes, ka, ha, None, row)
        for g in range(wb // LANES):
            lanes = slice(g * LANES, (g + 1) * LANES)

            def tile_b(i):
                if i < hb // SUBLANES:
                    return sb_ref[b, i * SUBLANES:(i + 1) * SUBLANES, lanes]
                return urbuf[rows, lanes]

            cbbuf[rows, lanes], _ = _conv_tile(tile_b, w["conv_b_w"], lanes, kb, hb, None, row)
        na_ref[b, 0:ka - 1 - steps, :] = sa_ref[b, steps:ka - 1, :]
        na_ref[b, ka - 1 - steps:ka - 1, :] = gabuf[rows, :]
        nb_ref[b] = urbuf[rows, :][steps - (kb - 1):steps, :]
        return carry

    lax.fori_loop(0, n_seq, conv_seq, 0)
    ya = _silu(_layer_norm(cabuf[...] + w["conv_a_b"][...], w["ln_g"][...], w["ln_b"][...]))
    _mixer_gates(w, cbbuf, abuf, bbuf, None)
    row_b = lax.broadcasted_iota(jnp.int32, (SUBLANES, wb), 0)

    def scan_seq(b, carry):
        rows = pl.ds(pl.multiple_of(b * SUBLANES, SUBLANES), SUBLANES)
        a_cum, b_cum = _scan_tile(abuf[rows, :], bbuf[rows, :], row_b)
        hs = a_cum * sh_ref[b] + b_cum
        bbuf[rows, :] = hs
        nh_ref[b] = hs[SUBLANES - 1:SUBLANES, :]
        return carry

    lax.fori_loop(0, n_seq, scan_seq, 0, unroll=2)
    yb = bbuf[...] * _gelu_tanh(_proj(w, h, 2 * wa + wb, 2 * wa + 2 * wb))
    x1 = _mixer_tail(w, x, per_token(2), ya, yb)
    h_ffn = _ffn_head(w, x1, per_token(3), per_token(4))
    _ffn_chunks(w, h_ffn, ffacc, 0, w["w_up"].shape[-1] // FF_CHUNK)
    out = _ffn_tail(w, x1, per_token(5), ffacc, False)
    xres[blk] = out
    xo_ref[...] = jnp.where(layer == pl.num_programs(0) - 1, _rms(out, w["g_final"][...]), out)


def _ada_body(c_ref, w_ref, b_ref, o_ref):
    c = c_ref[...]
    o_ref[...] = _dot(_silu(c).astype(BF16), w_ref[...].astype(BF16)) + b_ref[...]


def _ada_call(c_all, w_ada, b_ada):
    depth, d, n = w_ada.shape
    rows = c_all.shape[0]
    assert ADA_COLS == d
    return pl.pallas_call(
        _ada_body,
        grid=(depth, n // d),
        in_specs=[
            pl.BlockSpec((rows, d), lambda l, j: (0, 0)),
            pl.BlockSpec((None, d, d), lambda l, j: (l, 0, j)),
            pl.BlockSpec((None, 1, d), lambda l, j: (l, 0, j)),
        ],
        out_specs=pl.BlockSpec((None, None, rows, d), lambda l, j: (l, j, 0, 0)),
        out_shape=jax.ShapeDtypeStruct((depth, n // d, rows, d), F32),
        compiler_params=pltpu.CompilerParams(dimension_semantics=("arbitrary", "arbitrary")),
        name="ada_mod",
    )(c_all, w_ada, b_ada.reshape(depth, 1, n))


def _resident(arr, layer=None):
    tail = (0,) * (arr.ndim - 1)
    index_map = (lambda l, i: (l,) + tail) if layer is None else (lambda s: (layer,) + tail)
    return pl.BlockSpec((None,) + arr.shape[1:], index_map, pipeline_mode=pl.Buffered(1))


def _prompt_call(x, mod, weights, layer, dims, final_norm):
    batch, seq, d = x.shape
    ka, kb, ha, hb = dims
    wa = weights["conv_a_b"].shape[-1]
    wb = weights["conv_b_b"].shape[-1]
    r = PROMPT_ROWS
    per_seq = seq // r
    n_chunks = batch * per_seq
    ws = [weights[k] for k in _WEIGHT_NAMES]

    def mixer_chunk(s):
        return jnp.minimum(s, n_chunks - 1)

    def ffn_chunk(s):
        return jnp.maximum(s - 1, 0)

    return pl.pallas_call(
        functools.partial(_prompt_body, final_norm, dims, per_seq, n_chunks),
        grid=(n_chunks + 1,),
        in_specs=[
            pl.BlockSpec(memory_space=pltpu.SMEM),
            pl.BlockSpec((None, r, d), lambda s: (mixer_chunk(s) // per_seq, mixer_chunk(s) % per_seq, 0)),
            pl.BlockSpec((None, 6, d), lambda s: (mixer_chunk(s) // per_seq, 0, 0)),
            pl.BlockSpec((None, 6, d), lambda s: (ffn_chunk(s) // per_seq, 0, 0)),
        ] + [_resident(a, layer) for a in ws],
        out_specs=[
            pl.BlockSpec((None, r, d), lambda s: (ffn_chunk(s) // per_seq, ffn_chunk(s) % per_seq, 0)),
            pl.BlockSpec((None, ka - 1, wa), lambda s: (mixer_chunk(s) // per_seq, 0, 0)),
            pl.BlockSpec((None, kb - 1, wb), lambda s: (mixer_chunk(s) // per_seq, 0, 0)),
            pl.BlockSpec((None, 1, wb), lambda s: (mixer_chunk(s) // per_seq, 0, 0)),
        ],
        out_shape=[
            jax.ShapeDtypeStruct((batch, seq, d), F32),
            jax.ShapeDtypeStruct((batch, ka - 1, wa), F32),
            jax.ShapeDtypeStruct((batch, kb - 1, wb), F32),
            jax.ShapeDtypeStruct((batch, 1, wb), F32),
        ],
        scratch_shapes=[
            pltpu.VMEM((ha + r, wa), F32),
            pltpu.VMEM((hb + r, wb), F32),
            pltpu.VMEM((r, wa), F32),
            pltpu.VMEM((r, wb), F32),
            pltpu.VMEM((r, wb), F32),
            pltpu.VMEM((r, wb), F32),
            pltpu.VMEM((SUBLANES, wb), F32),
            pltpu.VMEM((r, d), BF16),
            pltpu.VMEM((r, d), BF16),
            pltpu.VMEM((r, wa), BF16),
            pltpu.VMEM((r, wb), F32),
            pltpu.VMEM((r, d), F32),
            pltpu.VMEM((r, d), F32),
            pltpu.VMEM((r, d), F32),
            pltpu.VMEM((r, FF_CHUNK), BF16),
            pltpu.VMEM((r, FF_CHUNK), BF16),
        ],
        compiler_params=pltpu.CompilerParams(
            dimension_semantics=("arbitrary",), vmem_limit_bytes=VMEM_LIMIT),
        name="prompt_layer",
    )(jnp.ones((1,), jnp.int32), x, mod, mod, *ws)


def _sample_call(x, mod, state_a, state_b, state_h, weights, dims):
    n_rows, d = x.shape
    depth, n_seq = state_a.shape[:2]
    steps = n_rows // n_seq
    ka, kb, ha, hb = dims
    wa = weights["conv_a_b"].shape[-1]
    wb = weights["conv_b_b"].shape[-1]
    bt = SAMPLE_SEQS
    r = bt * steps
    ws = [weights[k] for k in _WEIGHT_NAMES]

    def per_block(shape):
        return pl.BlockSpec((None, bt) + shape, lambda l, i: (l, i, 0, 0))

    return pl.pallas_call(
        functools.partial(_sample_body, dims),
        grid=(depth, n_seq // bt),
        in_specs=[
            pl.BlockSpec((r, d), lambda l, i: (i, 0)),
            pl.BlockSpec((None, 6, bt, d), lambda l, i: (l, 0, i, 0)),
            per_block((ka - 1, wa)),
            per_block((hb, wb)),
            per_block((1, wb)),
        ] + [_resident(a) for a in ws],
        out_specs=[
            pl.BlockSpec((None, r, d), lambda l, i: (l, i, 0)),
            per_block((ka - 1, wa)),
            per_block((kb - 1, wb)),
            per_block((1, wb)),
        ],
        out_shape=[
            jax.ShapeDtypeStruct((depth, n_rows, d), F32),
            jax.ShapeDtypeStruct((depth, n_seq, ka - 1, wa), F32),
            jax.ShapeDtypeStruct((depth, n_seq, kb - 1, wb), F32),
            jax.ShapeDtypeStruct((depth, n_seq, 1, wb), F32),
        ],
        scratch_shapes=[pltpu.VMEM((n_seq // bt, r, d), F32)]
        + [pltpu.VMEM((r, wa), F32), pltpu.VMEM((r, wb), F32)] * 3 + [pltpu.VMEM((r, d), F32)],
        compiler_params=pltpu.CompilerParams(
            dimension_semantics=("arbitrary", "arbitrary"), vmem_limit_bytes=VMEM_LIMIT),
        name="sample_layers",
    )(x, mod, state_a, state_b, state_h, *ws)


def _hist_rows(width):
    return -(-(width - 1) // SUBLANES) * SUBLANES


def _gate_blocks(w_ga, w_gx):
    depth, heads, hd, _ = w_ga.shape
    per_block = MXU_DIM // hd
    blocks = jnp.zeros((depth, heads // per_block, MXU_DIM, 2 * MXU_DIM), F32)
    for h in range(heads):
        k, i = divmod(h, per_block)
        rows = slice(i * hd, (i + 1) * hd)
        blocks = blocks.at[:, k, rows, i * hd:(i + 1) * hd].set(w_ga[:, h])
        blocks = blocks.at[:, k, rows, MXU_DIM + i * hd:MXU_DIM + (i + 1) * hd].set(w_gx[:, h])
    return blocks


def _stacked_weights(g_mix, w_in, b_in, conv_a_w, conv_a_b, ln_a_g, ln_a_b, conv_b_w, conv_b_b,
                     w_gate_a, b_gate_a, w_gate_x, b_gate_x, lru_lambda, w_out, g_ffn, w_up, w_down, g_final):
    depth = w_in.shape[0]

    def row(v):
        return v.reshape(depth, 1, -1)

    def taps(wk):
        return jnp.broadcast_to(wk[:, :, None, :], wk.shape[:2] + (SUBLANES, wk.shape[2]))

    return {
        "g_mix": row(g_mix), "w_in": _pack_rows(w_in), "b_in": row(b_in),
        "conv_a_w": taps(conv_a_w), "conv_a_b": row(conv_a_b), "ln_g": row(ln_a_g), "ln_b": row(ln_a_b),
        "conv_b_w": taps(conv_b_w), "conv_b_b": row(conv_b_b),
        "w_gate": _pack_rows(_gate_blocks(w_gate_a, w_gate_x)),
        "b_ga": row(b_gate_a), "b_gx": row(b_gate_x), "lam": row(lru_lambda),
        "w_out": _pack_rows(w_out), "g_ffn": row(g_ffn),
        "w_up": _pack_rows(w_up), "w_down": _pack_rows(w_down),
        "g_final": jnp.broadcast_to(g_final.reshape(1, 1, -1), (depth, 1, g_final.shape[-1])),
    }


def kernel(x_prompt, x_sample, c_prompt, c_sample, state_conv_a, state_conv_b, state_lru_h, w_ada, b_ada, g_mix, w_in, b_in, conv_a_w, conv_a_b, ln_a_g, ln_a_b, conv_b_w, conv_b_b, w_gate_a, b_gate_a, w_gate_x, b_gate_x, lru_lambda, w_out, g_ffn, w_up, w_down, g_final):
    batch, seq, d = x_prompt.shape
    n_seq, steps, _ = x_sample.shape
    depth = w_ada.shape[0]
    ka, wa = conv_a_w.shape[1:]
    kb, wb = conv_b_w.shape[1:]
    ha, hb = _hist_rows(ka), _hist_rows(kb)
    dims = (ka, kb, ha, hb)
    assert steps == SUBLANES and seq % PROMPT_ROWS == 0 and n_seq % SAMPLE_SEQS == 0
    assert wa % MXU_DIM == 0 and wb % MXU_DIM == 0 and MXU_DIM % w_gate_a.shape[-1] == 0

    mod = _ada_call(jnp.concatenate([c_sample, c_prompt], axis=0), w_ada, b_ada)
    mod_p = mod[:, :, n_seq:].transpose(0, 2, 1, 3)
    sb = jnp.pad(state_conv_b, ((0, 0), (0, 0), (hb - (kb - 1), 0), (0, 0)))
    weights = _stacked_weights(g_mix, w_in, b_in, conv_a_w, conv_a_b, ln_a_g, ln_a_b, conv_b_w, conv_b_b,
                               w_gate_a, b_gate_a, w_gate_x, b_gate_x, lru_lambda, w_out, g_ffn, w_up, w_down,
                               g_final)
    xs, qa, qb, qh = _sample_call(x_sample.reshape(n_seq * steps, d), mod, state_conv_a, sb,
                                  state_lru_h.reshape(depth, n_seq, 1, wb), weights, dims)
    xp = x_prompt
    outs_p = []
    for l in range(depth):
        xp, pa, pb, ph = _prompt_call(xp, mod_p[l], weights, l, dims, l == depth - 1)
        outs_p.append((pa, pb, ph.reshape(batch, wb)))

    def stack(outs, i):
        return jnp.stack([o[i] for o in outs])

    return (xp, xs[depth - 1].reshape(n_seq, steps, d),
            stack(outs_p, 0), stack(outs_p, 1), stack(outs_p, 2),
            qa, qb, qh.reshape(depth, n_seq, wb))
```

```python
import functools

import jax
import jax.numpy as jnp
from jax import lax
from jax.experimental import pallas as pl
from jax.experimental.pallas import tpu as pltpu

F32 = jnp.float32
BF16 = jnp.bfloat16

SUBLANES = 8
LANES = 128
MXU_DIM = 256

RG_C = 8.0
EPS = 1e-6
GELU_C = 0.7978845608028654

PROMPT_ROWS = 256
SAMPLE_SEQS = 32
FF_CHUNK = 512
ADA_COLS = 1024
PACK_ELEMS = 1 << 20
VMEM_LIMIT = 52 * 1024 * 1024


def _sigmoid(x):
    return 1.0 / (1.0 + jnp.exp(-x))


def _silu(x):
    return x * _sigmoid(x)


def _gelu_tanh(x):
    return x * (0.5 * (1.0 + jnp.tanh(GELU_C * (x + 0.044715 * (x * x * x)))))


def _softplus(x):
    return jnp.maximum(x, 0.0) + jnp.log1p(jnp.exp(-jnp.abs(x)))


def _rms(x, g):
    return x * lax.rsqrt(jnp.mean(x * x, axis=-1, keepdims=True) + EPS) * g


def _layer_norm(x, g, b):
    mu = jnp.mean(x, axis=-1, keepdims=True)
    xc = x - mu
    var = jnp.mean(xc * xc, axis=-1, keepdims=True)
    return xc * lax.rsqrt(var + EPS) * g + b


def _dot(a, b):
    return jnp.dot(a, b, preferred_element_type=F32)


def _pack_body(w_ref, o_ref):
    o_ref[...] = pltpu.bitcast(w_ref[...].astype(BF16), jnp.uint32)


def _pack_rows(w):
    k, n = w.shape[-2:]
    flat = w.reshape(-1, n)
    rows = flat.shape[0]
    block = min(rows, PACK_ELEMS // n)
    assert k % 2 == 0 and rows % block == 0 and block % (2 * SUBLANES) == 0
    packed = pl.pallas_call(
        _pack_body,
        grid=(rows // block,),
        in_specs=[pl.BlockSpec((block, n), lambda i: (i, 0))],
        out_specs=pl.BlockSpec((block // 2, n), lambda i: (i, 0)),
        out_shape=jax.ShapeDtypeStruct((rows // 2, n), jnp.uint32),
        compiler_params=pltpu.CompilerParams(dimension_semantics=("arbitrary",)),
        name="pack_weights",
    )(flat)
    return packed.reshape(w.shape[:-2] + (k // 2, n))


def _wmat(ref, rows=slice(None), cols=slice(None), lead=()):
    start = None if rows.start is None else rows.start // 2
    stop = None if rows.stop is None else rows.stop // 2
    return pltpu.bitcast(ref[lead + (slice(start, stop), cols)], BF16)


def _conv_taps(width, hist_rows):
    lead = hist_rows - (width - 1)
    taps = {}
    for k in range(width):
        taps.setdefault((k + lead) % SUBLANES, []).append((k + lead) // SUBLANES)
    return lead, taps


def _conv_tile(tile_fn, w_ref, lanes, width, hist_rows, prev, row):
    lead, taps = _conv_taps(width, hist_rows)

    def z(r, off):
        acc = None
        for q in taps[r]:
            term = w_ref[SUBLANES * q + r - lead, :, lanes] * tile_fn(q + off)
            acc = term if acc is None else acc + term
        return acc

    shifted = [r for r in sorted(taps) if r != 0]
    new = [z(r, 1) for r in shifted]
    if prev is None:
        prev = [z(r, 0) for r in shifted]
    out = z(0, 0)
    for r, p, n in zip(shifted, prev, new):
        out = out + pltpu.roll(jnp.where(row >= r, p, n), SUBLANES - r, axis=0)
    return out, new


def _conv_steps(buf, w_ref, out_ref, n_rows, width, hist_rows, g):
    row = lax.broadcasted_iota(jnp.int32, (SUBLANES, LANES), 0)
    lanes = slice(g * LANES, (g + 1) * LANES)
    carry = {"prev": None}

    def make(j):
        def step():
            def tile_fn(i):
                return buf[(j + i) * SUBLANES:(j + i + 1) * SUBLANES, lanes]

            out, carry["prev"] = _conv_tile(tile_fn, w_ref, lanes, width, hist_rows, carry["prev"], row)
            out_ref[j * SUBLANES:(j + 1) * SUBLANES, lanes] = out

        return step

    return [make(j) for j in range(n_rows // SUBLANES)]


def _alternate(pieces, steps):
    total = sum(c for _, c in pieces)
    done = 0
    cost = 0
    for thunk, c in pieces:
        upto = len(steps) * (2 * cost + c) // (2 * total)
        for step in steps[done:upto]:
            step()
        done = upto
        thunk()
        cost += c
    for step in steps[done:]:
        step()


def _scan_tile(a, b, row):
    for s in (1, 2, 4):
        keep = row >= s
        a_prev = jnp.where(keep, pltpu.roll(a, s, axis=0), 1.0)
        b_prev = jnp.where(keep, pltpu.roll(b, s, axis=0), 0.0)
        b = a * b_prev + b
        a = a * a_prev
    return a, b


def _scan_rows(a_ref, b_ref, carry_ref, n_rows):
    channels = a_ref.shape[-1]
    row = lax.broadcasted_iota(jnp.int32, (SUBLANES, channels), 0)
    h_prev = carry_ref[...]
    for j in range(n_rows // SUBLANES):
        rows = slice(j * SUBLANES, (j + 1) * SUBLANES)
        a_cum, b_cum = _scan_tile(a_ref[rows, :], b_ref[rows, :], row)
        h = a_cum * h_prev + b_cum
        b_ref[rows, :] = h
        h_prev = jnp.broadcast_to(h[SUBLANES - 1:SUBLANES, :], (SUBLANES, channels))
    carry_ref[...] = h_prev


def _lru_inputs(cb, g, b_ga, b_gx, softplus_neg_lam, first_row):
    n = cb.shape[-1]
    gate_a = g[:, 0:n] + b_ga
    gate_x = g[:, n:2 * n] + b_gx
    log_a = -RG_C * _sigmoid(gate_a) * softplus_neg_lam
    a = jnp.exp(log_a)
    mult = jnp.sqrt(-jnp.tanh(log_a) * (a * a + 1.0))
    if first_row is not None:
        mult = jnp.where(first_row, 1.0, mult)
    return a, mult * _sigmoid(gate_x) * cb


def _mixer_head(w, x, shift, scale):
    return (_rms(x, w["g_mix"][...]) * (1.0 + scale) + shift).astype(BF16)


def _proj(w, h, c0, c1):
    return _dot(h, _wmat(w["w_in"], cols=slice(c0, c1))) + w["b_in"][:, c0:c1]


def _mixer_gates(w, cb_ref, a_ref, b_ref, first_row):
    cb = cb_ref[...] + w["conv_b_b"][...]
    cb_lo = cb.astype(BF16)
    sp = _softplus(-w["lam"][...])
    for kb in range(cb.shape[-1] // MXU_DIM):
        cs = slice(kb * MXU_DIM, (kb + 1) * MXU_DIM)
        g = _dot(cb_lo[:, cs], _wmat(w["w_gate"], lead=(kb,)))
        a, b = _lru_inputs(cb[:, cs], g, w["b_ga"][:, cs], w["b_gx"][:, cs], sp[:, cs], first_row)
        a_ref[:, cs] = a
        b_ref[:, cs] = b


def _mixer_tail(w, x, gate, ya, yb):
    wa = ya.shape[-1]
    wy = yb.shape[-1]
    mix = (_dot(ya.astype(BF16), _wmat(w["w_out"], rows=slice(0, wa)))
           + _dot(yb.astype(BF16), _wmat(w["w_out"], rows=slice(wa, wa + wy))))
    return x + gate * mix


def _ffn_head(w, x, shift, scale):
    return (_rms(x, w["g_ffn"][...]) * (1.0 + scale) + shift).astype(BF16)


def _ffn_chunks(w, h, acc_ref, first, last):
    for j in range(first, last):
        cols = slice(j * FF_CHUNK, (j + 1) * FF_CHUNK)
        act = jnp.square(jnp.maximum(_dot(h, _wmat(w["w_up"], cols=cols)), 0.0)).astype(BF16)
        part = _dot(act, _wmat(w["w_down"], rows=cols))
        if j == 0:
            acc_ref[...] = part
        else:
            acc_ref[...] += part


def _ffn_pieces(w, hf_ref, act_ref, acc_ref, j):
    cols = slice(j * FF_CHUNK, (j + 1) * FF_CHUNK)
    half = acc_ref.shape[-1] // 2

    def up():
        act_ref[...] = jnp.square(jnp.maximum(_dot(hf_ref[...], _wmat(w["w_up"], cols=cols)), 0.0)).astype(BF16)

    def down(n):
        def piece():
            c = slice(n * half, (n + 1) * half)
            part = _dot(act_ref[...], _wmat(w["w_down"], rows=cols, cols=c))
            if j == 0:
                acc_ref[:, c] = part
            else:
                acc_ref[:, c] += part

        return piece

    return [(up, 2), (down(0), 1), (down(1), 1)]


def _ffn_tail(w, x, gate, acc_ref, final_norm):
    out = x + gate * acc_ref[...]
    if final_norm:
        out = _rms(out, w["g_final"][...])
    return out


_WEIGHT_NAMES = ("g_mix", "w_in", "b_in", "conv_a_w", "conv_a_b", "ln_g", "ln_b", "conv_b_w", "conv_b_b",
                 "w_gate", "b_ga", "b_gx", "lam", "w_out", "g_ffn", "w_up", "w_down", "g_final")


def _prompt_body(final_norm, dims, chunks_per_seq, n_chunks, go_ref, x_ref, mod_ref, mod_ffn_ref, *rest):
    ka, kb, ha, hb = dims
    nw = len(_WEIGHT_NAMES)
    w = dict(zip(_WEIGHT_NAMES, rest[:nw]))
    (xo_ref, na_ref, nb_ref, nh_ref, gbuf, rbuf, cabuf, cbbuf, abuf, bbuf, hcar,
     hbuf, hfbuf, yabuf, gybuf, x1_new, x1_old, ffacc, act0, act1) = rest[nw:]
    acts = (act0, act1)
    n_rows = x_ref.shape[0]
    wa = gbuf.shape[-1]
    wb = rbuf.shape[-1]
    s = pl.program_id(0)
    t = jnp.minimum(s, n_chunks - 1) % chunks_per_seq
    n_ff = w["w_up"].shape[-1] // FF_CHUNK
    assert n_ff == 8

    @pl.when(s == 0)
    def _():
        x1_old[...] = jnp.zeros_like(x1_old)
        hfbuf[...] = jnp.zeros_like(hfbuf)

    @pl.when(t == 0)
    def _():
        gbuf[0:ha, :] = jnp.zeros((ha, wa), F32)
        rbuf[0:hb, :] = jnp.zeros((hb, wb), F32)
        hcar[...] = jnp.zeros_like(hcar)

    h = _mixer_head(w, x_ref[...], mod_ref[0:1, :], mod_ref[1:2, :])
    hbuf[...] = h
    gbuf[ha:ha + n_rows, :] = _proj(w, h, 0, wa) * _sigmoid(_proj(w, h, wa, 2 * wa))
    rbuf[hb:hb + n_rows, :] = _proj(w, h, 2 * wa, 2 * wa + wb)

    @pl.when(go_ref[0] != 0)
    def _():
        steps = []
        for g in range(wa // LANES):
            conv_a = _conv_steps(gbuf, w["conv_a_w"], cabuf, n_rows, ka, ha, g)
            conv_b = _conv_steps(rbuf, w["conv_b_w"], cbbuf, n_rows, kb, hb, g)
            steps += [f for pair in zip(conv_a, conv_b) for f in pair]
        f0, f1, f2, f3, f4 = (_ffn_pieces(w, hfbuf, acts[j % 2], ffacc, j) for j in range(5))
        order = [f0[0], f1[0], f0[1], f0[2], f2[0], f1[1], f1[2], f3[0], f2[1], f2[2],
                 f4[0], f3[1], f3[2], f4[1], f4[2]]
        _alternate(order, steps)

    ya = _silu(_layer_norm(cabuf[...] + w["conv_a_b"][...], w["ln_g"][...], w["ln_b"][...]))
    yabuf[...] = ya.astype(BF16)
    first_row = (lax.broadcasted_iota(jnp.int32, (n_rows, 1), 0) + t * n_rows) == 0
    _mixer_gates(w, cbbuf, abuf, bbuf, first_row)
    gybuf[...] = _gelu_tanh(_proj(w, hbuf[...], 2 * wa + wb, 2 * wa + 2 * wb))
    _ffn_chunks(w, hfbuf[...], ffacc, 5, 6)
    _scan_rows(abuf, bbuf, hcar, n_rows)
    yb = bbuf[...] * gybuf[...]
    x1_new[...] = _mixer_tail(w, x_ref[...], mod_ref[2:3, :], yabuf[...], yb)
    _ffn_chunks(w, hfbuf[...], ffacc, 6, n_ff)
    xo_ref[...] = _ffn_tail(w, x1_old[...], mod_ffn_ref[5:6, :], ffacc, final_norm)
    x1_old[...] = x1_new[...]
    hfbuf[...] = _ffn_head(w, x1_new[...], mod_ref[3:4, :], mod_ref[4:5, :])
    gbuf[0:ha, :] = gbuf[n_rows:n_rows + ha, :]
    rbuf[0:hb, :] = rbuf[n_rows:n_rows + hb, :]

    @pl.when((t == chunks_per_seq - 1) & (s < n_chunks))
    def _():
        na_ref[...] = gbuf[ha - (ka - 1):ha, :]
        nb_ref[...] = rbuf[hb - (kb - 1):hb, :]
        nh_ref[...] = hcar[0:1, :]


def _sample_body(dims, x_ref, mod_ref, sa_ref, sb_ref, sh_ref, *rest):
    ka, kb, ha, hb = dims
    nw = len(_WEIGHT_NAMES)
    w = dict(zip(_WEIGHT_NAMES, rest[:nw]))
    xo_ref, na_ref, nb_ref, nh_ref, xres, gabuf, urbuf, cabuf, cbbuf, abuf, bbuf, ffacc = rest[nw:]
    layer = pl.program_id(0)
    blk = pl.program_id(1)
    n_rows, d = x_ref.shape
    n_seq = sa_ref.shape[0]
    steps = n_rows // n_seq
    wa = gabuf.shape[-1]
    wb = urbuf.shape[-1]
    lead_a = ha - (ka - 1)

    @pl.when(layer == 0)
    def _():
        xres[blk] = x_ref[...]

    def per_token(i):
        return jnp.broadcast_to(mod_ref[i][:, None, :], (n_seq, steps, d)).reshape(n_rows, d)

    x = xres[blk]
    h = _mixer_head(w, x, per_token(0), per_token(1))
    gabuf[...] = _proj(w, h, 0, wa) * _sigmoid(_proj(w, h, wa, 2 * wa))
    urbuf[...] = _proj(w, h, 2 * wa, 2 * wa + wb)
    row = lax.broadcasted_iota(jnp.int32, (SUBLANES, LANES), 0)

    def conv_seq(b, carry):
        rows = pl.ds(pl.multiple_of(b * SUBLANES, SUBLANES), SUBLANES)
        for g in range(wa // LANES):
            lanes = slice(g * LANES, (g + 1) * LANES)

            def tile_a(i):
                if i == 0:
                    return pltpu.roll(sa_ref[b, 0:SUBLANES, lanes], lead_a, axis=0)
                if i < ha // SUBLANES:
                    return sa_ref[b, i * SUBLANES - lead_a:(i + 1) * SUBLANES - lead_a, lanes]
                return gabuf[rows, lanes]

            cabuf[rows, lanes], _ = _conv_tile(tile_a, w["conv_a_w"], lanes, ka, ha, None, row)
        for g in range(wb // LANES):
            lanes = slice(g * LANES, (g + 1) * LANES)

            def tile_b(i):
                if i < hb // SUBLANES:
                    return sb_ref[b, i * SUBLANES:(i + 1) * SUBLANES, lanes]
                return urbuf[rows, lanes]

            cbbuf[rows, lanes], _ = _conv_tile(tile_b, w["conv_b_w"], lanes, kb, hb, None, row)
        na_ref[b, 0:ka - 1 - steps, :] = sa_ref[b, steps:ka - 1, :]
        na_ref[b, ka - 1 - steps:ka - 1, :] = gabuf[rows, :]
        nb_ref[b] = urbuf[rows, :][steps - (kb - 1):steps, :]
        return carry

    lax.fori_loop(0, n_seq, conv_seq, 0)
    ya = _silu(_layer_norm(cabuf[...] + w["conv_a_b"][...], w["ln_g"][...], w["ln_b"][...]))
    _mixer_gates(w, cbbuf, abuf, bbuf, None)
    row_b = lax.broadcasted_iota(jnp.int32, (SUBLANES, wb), 0)

    def scan_seq(b, carry):
        rows = pl.ds(pl.multiple_of(b * SUBLANES, SUBLANES), SUBLANES)
        a_cum, b_cum = _scan_tile(abuf[rows, :], bbuf[rows, :], row_b)
        hs = a_cum * sh_ref[b] + b_cum
        bbuf[rows, :] = hs
        nh_ref[b] = hs[SUBLANES - 1:SUBLANES, :]
        return carry

    lax.fori_loop(0, n_seq, scan_seq, 0, unroll=2)
    yb = bbuf[...] * _gelu_tanh(_proj(w, h, 2 * wa + wb, 2 * wa + 2 * wb))
    x1 = _mixer_tail(w, x, per_token(2), ya, yb)
    h_ffn = _ffn_head(w, x1, per_token(3), per_token(4))
    _ffn_chunks(w, h_ffn, ffacc, 0, w["w_up"].shape[-1] // FF_CHUNK)
    out = _ffn_tail(w, x1, per_token(5), ffacc, False)
    xres[blk] = out
    xo_ref[...] = jnp.where(layer == pl.num_programs(0) - 1, _rms(out, w["g_final"][...]), out)


def _ada_body(c_ref, w_ref, b_ref, o_ref):
    c = c_ref[...]
    o_ref[...] = _dot(_silu(c).astype(BF16), w_ref[...].astype(BF16)) + b_ref[...]


def _ada_call(c_all, w_ada, b_ada):
    depth, d, n = w_ada.shape
    rows = c_all.shape[0]
    assert ADA_COLS == d
    return pl.pallas_call(
        _ada_body,
        grid=(depth, n // d),
        in_specs=[
            pl.BlockSpec((rows, d), lambda l, j: (0, 0)),
            pl.BlockSpec((None, d, d), lambda l, j: (l, 0, j)),
            pl.BlockSpec((None, 1, d), lambda l, j: (l, 0, j)),
        ],
        out_specs=pl.BlockSpec((None, None, rows, d), lambda l, j: (l, j, 0, 0)),
        out_shape=jax.ShapeDtypeStruct((depth, n // d, rows, d), F32),
        compiler_params=pltpu.CompilerParams(dimension_semantics=("arbitrary", "arbitrary")),
        name="ada_mod",
    )(c_all, w_ada, b_ada.reshape(depth, 1, n))


def _resident(arr, layer=None):
    tail = (0,) * (arr.ndim - 1)
    index_map = (lambda l, i: (l,) + tail) if layer is None else (lambda s: (layer,) + tail)
    return pl.BlockSpec((None,) + arr.shape[1:], index_map, pipeline_mode=pl.Buffered(1))


def _prompt_call(x, mod, weights, layer, dims, final_norm):
    batch, seq, d = x.shape
    ka, kb, ha, hb = dims
    wa = weights["conv_a_b"].shape[-1]
    wb = weights["conv_b_b"].shape[-1]
    r = PROMPT_ROWS
    per_seq = seq // r
    n_chunks = batch * per_seq
    ws = [weights[k] for k in _WEIGHT_NAMES]

    def mixer_chunk(s):
        return jnp.minimum(s, n_chunks - 1)

    def ffn_chunk(s):
        return jnp.maximum(s - 1, 0)

    return pl.pallas_call(
        functools.partial(_prompt_body, final_norm, dims, per_seq, n_chunks),
        grid=(n_chunks + 1,),
        in_specs=[
            pl.BlockSpec(memory_space=pltpu.SMEM),
            pl.BlockSpec((None, r, d), lambda s: (mixer_chunk(s) // per_seq, mixer_chunk(s) % per_seq, 0)),
            pl.BlockSpec((None, 6, d), lambda s: (mixer_chunk(s) // per_seq, 0, 0)),
            pl.BlockSpec((None, 6, d), lambda s: (ffn_chunk(s) // per_seq, 0, 0)),
        ] + [_resident(a, layer) for a in ws],
        out_specs=[
            pl.BlockSpec((None, r, d), lambda s: (ffn_chunk(s) // per_seq, ffn_chunk(s) % per_seq, 0)),
            pl.BlockSpec((None, ka - 1, wa), lambda s: (mixer_chunk(s) // per_seq, 0, 0)),
            pl.BlockSpec((None, kb - 1, wb), lambda s: (mixer_chunk(s) // per_seq, 0, 0)),
            pl.BlockSpec((None, 1, wb), lambda s: (mixer_chunk(s) // per_seq, 0, 0)),
        ],
        out_shape=[
            jax.ShapeDtypeStruct((batch, seq, d), F32),
            jax.ShapeDtypeStruct((batch, ka - 1, wa), F32),
            jax.ShapeDtypeStruct((batch, kb - 1, wb), F32),
            jax.ShapeDtypeStruct((batch, 1, wb), F32),
        ],
        scratch_shapes=[
            pltpu.VMEM((ha + r, wa), F32),
            pltpu.VMEM((hb + r, wb), F32),
            pltpu.VMEM((r, wa), F32),
            pltpu.VMEM((r, wb), F32),
            pltpu.VMEM((r, wb), F32),
            pltpu.VMEM((r, wb), F32),
            pltpu.VMEM((SUBLANES, wb), F32),
            pltpu.VMEM((r, d), BF16),
            pltpu.VMEM((r, d), BF16),
            pltpu.VMEM((r, wa), BF16),
            pltpu.VMEM((r, wb), F32),
            pltpu.VMEM((r, d), F32),
            pltpu.VMEM((r, d), F32),
            pltpu.VMEM((r, d), F32),
            pltpu.VMEM((r, FF_CHUNK), BF16),
            pltpu.VMEM((r, FF_CHUNK), BF16),
        ],
        compiler_params=pltpu.CompilerParams(
            dimension_semantics=("arbitrary",), vmem_limit_bytes=VMEM_LIMIT),
        name="prompt_layer",
    )(jnp.ones((1,), jnp.int32), x, mod, mod, *ws)


def _sample_call(x, mod, state_a, state_b, state_h, weights, dims):
    n_rows, d = x.shape
    depth, n_seq = state_a.shape[:2]
    steps = n_rows // n_seq
    ka, kb, ha, hb = dims
    wa = weights["conv_a_b"].shape[-1]
    wb = weights["conv_b_b"].shape[-1]
    bt = SAMPLE_SEQS
    r = bt * steps
    ws = [weights[k] for k in _WEIGHT_NAMES]

    def per_block(shape):
        return pl.BlockSpec((None, bt) + shape, lambda l, i: (l, i, 0, 0))

    return pl.pallas_call(
        functools.partial(_sample_body, dims),
        grid=(depth, n_seq // bt),
        in_specs=[
            pl.BlockSpec((r, d), lambda l, i: (i, 0)),
            pl.BlockSpec((None, 6, bt, d), lambda l, i: (l, 0, i, 0)),
            per_block((ka - 1, wa)),
            per_block((hb, wb)),
            per_block((1, wb)),
        ] + [_resident(a) for a in ws],
        out_specs=[
            pl.BlockSpec((None, r, d), lambda l, i: (l, i, 0)),
            per_block((ka - 1, wa)),
            per_block((kb - 1, wb)),
            per_block((1, wb)),
        ],
        out_shape=[
            jax.ShapeDtypeStruct((depth, n_rows, d), F32),
            jax.ShapeDtypeStruct((depth, n_seq, ka - 1, wa), F32),
            jax.ShapeDtypeStruct((depth, n_seq, kb - 1, wb), F32),
            jax.ShapeDtypeStruct((depth, n_seq, 1, wb), F32),
        ],
        scratch_shapes=[pltpu.VMEM((n_seq // bt, r, d), F32)]
        + [pltpu.VMEM((r, wa), F32), pltpu.VMEM((r, wb), F32)] * 3 + [pltpu.VMEM((r, d), F32)],
        compiler_params=pltpu.CompilerParams(
            dimension_semantics=("arbitrary", "arbitrary"), vmem_limit_bytes=VMEM_LIMIT),
        name="sample_layers",
    )(x, mod, state_a, state_b, state_h, *ws)


def _hist_rows(width):
    return -(-(width - 1) // SUBLANES) * SUBLANES


def _gate_blocks(w_ga, w_gx):
    depth, heads, hd, _ = w_ga.shape
    per_block = MXU_DIM // hd
    blocks = jnp.zeros((depth, heads // per_block, MXU_DIM, 2 * MXU_DIM), F32)
    for h in range(heads):
        k, i = divmod(h, per_block)
        rows = slice(i * hd, (i + 1) * hd)
        blocks = blocks.at[:, k, rows, i * hd:(i + 1) * hd].set(w_ga[:, h])
        blocks = blocks.at[:, k, rows, MXU_DIM + i * hd:MXU_DIM + (i + 1) * hd].set(w_gx[:, h])
    return blocks


def _stacked_weights(g_mix, w_in, b_in, conv_a_w, conv_a_b, ln_a_g, ln_a_b, conv_b_w, conv_b_b,
                     w_gate_a, b_gate_a, w_gate_x, b_gate_x, lru_lambda, w_out, g_ffn, w_up, w_down, g_final):
    depth = w_in.shape[0]

    def row(v):
        return v.reshape(depth, 1, -1)

    def taps(wk):
        return jnp.broadcast_to(wk[:, :, None, :], wk.shape[:2] + (SUBLANES, wk.shape[2]))

    return {
        "g_mix": row(g_mix), "w_in": _pack_rows(w_in), "b_in": row(b_in),
        "conv_a_w": taps(conv_a_w), "conv_a_b": row(conv_a_b), "ln_g": row(ln_a_g), "ln_b": row(ln_a_b),
        "conv_b_w": taps(conv_b_w), "conv_b_b": row(conv_b_b),
        "w_gate": _pack_rows(_gate_blocks(w_gate_a, w_gate_x)),
        "b_ga": row(b_gate_a), "b_gx": row(b_gate_x), "lam": row(lru_lambda),
        "w_out": _pack_rows(w_out), "g_ffn": row(g_ffn),
        "w_up": _pack_rows(w_up), "w_down": _pack_rows(w_down),
        "g_final": jnp.broadcast_to(g_final.reshape(1, 1, -1), (depth, 1, g_final.shape[-1])),
    }


def kernel(x_prompt, x_sample, c_prompt, c_sample, state_conv_a, state_conv_b, state_lru_h, w_ada, b_ada, g_mix, w_in, b_in, conv_a_w, conv_a_b, ln_a_g, ln_a_b, conv_b_w, conv_b_b, w_gate_a, b_gate_a, w_gate_x, b_gate_x, lru_lambda, w_out, g_ffn, w_up, w_down, g_final):
    batch, seq, d = x_prompt.shape
    n_seq, steps, _ = x_sample.shape
    depth = w_ada.shape[0]
    ka, wa = conv_a_w.shape[1:]
    kb, wb = conv_b_w.shape[1:]
    ha, hb = _hist_rows(ka), _hist_rows(kb)
    dims = (ka, kb, ha, hb)
    assert steps == SUBLANES and seq % PROMPT_ROWS == 0 and n_seq % SAMPLE_SEQS == 0
    assert wa % MXU_DIM == 0 and wb % MXU_DIM == 0 and MXU_DIM % w_gate_a.shape[-1] == 0

    mod = _ada_call(jnp.concatenate([c_sample, c_prompt], axis=0), w_ada, b_ada)
    mod_p = mod[:, :, n_seq:].transpose(0, 2, 1, 3)
    sb = jnp.pad(state_conv_b, ((0, 0), (0, 0), (hb - (kb - 1), 0), (0, 0)))
    weights = _stacked_weights(g_mix, w_in, b_in, conv_a_w, conv_a_b, ln_a_g, ln_a_b, conv_b_w, conv_b_b,
                               w_gate_a, b_gate_a, w_gate_x, b_gate_x, lru_lambda, w_out, g_ffn, w_up, w_down,
                               g_final)
    xs, qa, qb, qh = _sample_call(x_sample.reshape(n_seq * steps, d), mod, state_conv_a, sb,
                                  state_lru_h.reshape(depth, n_seq, 1, wb), weights, dims)
    xp = x_prompt
    outs_p = []
    for l in range(depth):
        xp, pa, pb, ph = _prompt_call(xp, mod_p[l], weights, l, dims, l == depth - 1)
        outs_p.append((pa, pb, ph.reshape(batch, wb)))

    def stack(outs, i):
        return jnp.stack([o[i] for o in outs])

    return (xp, xs[depth - 1].reshape(n_seq, steps, d),
            stack(outs_p, 0), stack(outs_p, 1), stack(outs_p, 2),
            qa, qb, qh.reshape(depth, n_seq, wb))
```
